```python
import jax, jax.numpy as jnp
from jax import lax
import numpy as np

D_MODEL = 4096
BATCH = 1
SEQ = 8192
DEPTH = 2
DEC_BATCH = 32
DEC_SEQ = 32
PAST_LEN = 2048

CHUNK = 64
N_MIXERS = 2
N_A_LAYERS = (DEPTH + N_MIXERS - 1) // N_MIXERS
N_B_LAYERS = DEPTH // N_MIXERS

GMLP_CHUNK = 128
GMLP_WIDTH = D_MODEL
GMLP_GROUPS = 8
GMLP_GROUP_DIM = GMLP_WIDTH // GMLP_GROUPS

MLA_HEADS = 32
Q_LORA = 1024
KV_LORA = 512
NOPE_DIM = 128
ROPE_DIM = 64
V_DIM = 128
ROPE_THETA = 10000.0
QUERY_BLOCK = 128
SOFTMAX_SCALE = (NOPE_DIM + ROPE_DIM) ** -0.5
NEG_INF = -1e30

N_EXPERTS = 32
TOP_K = 4
EXPERT_FF = 2048
SWIGLU_LIMIT = 7.0
SWIGLU_ALPHA = 1.702
MOE_BLOCK = 256

ALPHA = (2 * DEPTH) ** 0.25
BETA = (8 * DEPTH) ** -0.25
LN_EPS = 1e-5
RMS_EPS = 1e-6

kernel_name = 'hybrid_gmlp_mla_moe_stream_step'


def layer_norm(x, g, b):
    xf = x.astype(jnp.float32)
    mu = jnp.mean(xf, axis=-1, keepdims=True)
    var = jnp.mean(jnp.square(xf - mu), axis=-1, keepdims=True)
    y = (xf - mu) * lax.rsqrt(var + LN_EPS) * g.astype(jnp.float32) + b.astype(jnp.float32)
    return y.astype(x.dtype)


def rms_norm(x, g):
    xf = x.astype(jnp.float32)
    y = xf * lax.rsqrt(jnp.mean(jnp.square(xf), axis=-1, keepdims=True) + RMS_EPS) * g.astype(jnp.float32)
    return y.astype(x.dtype)


def rope(x, pos):
    half = ROPE_DIM // 2
    inv_freq = ROPE_THETA ** (-jnp.arange(half, dtype=jnp.float32) / half)
    ang = pos.astype(jnp.float32)[:, None] * inv_freq[None, :]
    ang = ang.reshape((1, pos.shape[0]) + (1,) * (x.ndim - 3) + (half,))
    cos = jnp.cos(ang).astype(x.dtype)
    sin = jnp.sin(ang).astype(x.dtype)
    x1, x2 = x[..., :half], x[..., half:]
    return jnp.concatenate([x1 * cos - x2 * sin, x2 * cos + x1 * sin], axis=-1)


def gmlp_mixer(x, w_in, ln_g, ln_b, w_s, b_s, w_out):
    bsz, s_len, _ = x.shape
    z = jax.nn.gelu(x @ w_in, approximate=False)
    u, v = z[..., :GMLP_WIDTH], z[..., GMLP_WIDTH:]
    v = layer_norm(v, ln_g, ln_b)
    blk = min(s_len, GMLP_CHUNK)
    n_chunks = s_len // blk
    tri = jnp.tril(jnp.ones((blk, blk), dtype=bool))
    w = jnp.where(tri[None], w_s[:, :blk, :blk], 0.0).astype(v.dtype)
    vg = v.reshape(bsz, n_chunks, blk, GMLP_GROUPS, GMLP_GROUP_DIM)
    s = jnp.einsum('gts,bnsgc->bntgc', w, vg) + b_s[:, :blk].T[None, None, :, :, None].astype(v.dtype)
    y = (u * s.reshape(bsz, s_len, GMLP_WIDTH)) @ w_out
    return y, v


def mla_project(x, pos, w_in, q_norm_g, kv_norm_g, w_uq):
    h = x @ w_in
    c_q = rms_norm(h[..., :Q_LORA], q_norm_g)
    c_kv = rms_norm(h[..., Q_LORA:Q_LORA + KV_LORA], kv_norm_g)
    k_rope = rope(h[..., Q_LORA + KV_LORA:], pos)
    q = jnp.einsum('bsl,lhd->bshd', c_q, w_uq)
    q_nope = q[..., :NOPE_DIM]
    q_rope = rope(q[..., NOPE_DIM:], pos)
    return q_nope, q_rope, c_kv, k_rope


def chunk_causal_softmax(scores, q_pos, k_pos):
    s = scores.astype(jnp.float32) * SOFTMAX_SCALE
    allowed = (k_pos[None, :] // CHUNK) <= (q_pos[:, None] // CHUNK)
    s = jnp.where(allowed, s, NEG_INF)
    return jax.nn.softmax(s, axis=-1)


def mla_prompt_attention(q_nope, q_rope, c_kv, k_rope, pos, w_uk, w_uv):
    bsz, s_len, n_h, _ = q_nope.shape
    k_nope = jnp.einsum('bkl,lhn->bkhn', c_kv, w_uk)
    v = jnp.einsum('bkl,lhv->bkhv', c_kv, w_uv)
    nb = s_len // QUERY_BLOCK

    def block(args):
        qn, qr, qp = args
        sc = jnp.einsum('bqhn,bkhn->bhqk', qn, k_nope) + jnp.einsum('bqhr,bkr->bhqk', qr, k_rope)
        p = chunk_causal_softmax(sc, qp, pos)
        return jnp.einsum('bhqk,bkhv->bqhv', p.astype(v.dtype), v)

    def to_blocks(t):
        return t.reshape((bsz, nb, QUERY_BLOCK) + t.shape[2:]).swapaxes(0, 1)

    o = lax.map(block, (to_blocks(q_nope), to_blocks(q_rope), pos.reshape(nb, QUERY_BLOCK)))
    return o.swapaxes(0, 1).reshape(bsz, s_len, n_h * V_DIM)


def mla_sample_attention(q_nope, q_rope, q_pos, c_kv_all, k_rope_all, k_pos, w_uk, w_uv):
    bsz, q_len, n_h, _ = q_nope.shape
    q_lat = jnp.einsum('bqhn,lhn->bqhl', q_nope, w_uk)
    sc = jnp.einsum('bqhl,bkl->bhqk', q_lat, c_kv_all) + jnp.einsum('bqhr,bkr->bhqk', q_rope, k_rope_all)
    p = chunk_causal_softmax(sc, q_pos, k_pos)
    o_lat = jnp.einsum('bhqk,bkl->bqhl', p.astype(c_kv_all.dtype), c_kv_all)
    o = jnp.einsum('bqhl,lhv->bqhv', o_lat, w_uv)
    return o.reshape(bsz, q_len, n_h * V_DIM)


def clamped_swiglu(h):
    gate = jnp.minimum(h[..., :EXPERT_FF], SWIGLU_LIMIT)
    up = jnp.clip(h[..., EXPERT_FF:], -SWIGLU_LIMIT, SWIGLU_LIMIT)
    return (up + 1.0) * gate * jax.nn.sigmoid(SWIGLU_ALPHA * gate)


def moe_ffn(x, router_w, router_b, w_gu, b_gu, w_down, b_down):
    shp = x.shape
    d = shp[-1]
    xt = x.reshape(-1, d)
    n_tok = xt.shape[0]
    logits = (xt @ router_w + router_b).astype(jnp.float32)
    top_val, top_idx = lax.top_k(logits, TOP_K)
    gate = jax.nn.softmax(top_val, axis=-1)
    n_assign = n_tok * TOP_K
    n_blocks = -(-n_assign // MOE_BLOCK) + N_EXPERTS
    flat_e = top_idx.reshape(-1).astype(jnp.int32)
    order = jnp.argsort(flat_e)
    e_sorted = flat_e[order]
    counts = jnp.bincount(flat_e, length=N_EXPERTS).astype(jnp.int32)
    starts = jnp.cumsum(counts) - counts
    padded = (counts + MOE_BLOCK - 1) // MOE_BLOCK * MOE_BLOCK
    pad_end = jnp.cumsum(padded)
    pad_start = pad_end - padded
    dest = pad_start[e_sorted] + jnp.arange(n_assign, dtype=jnp.int32) - starts[e_sorted]
    slot_tok = jnp.full((n_blocks * MOE_BLOCK,), n_tok, jnp.int32).at[dest].set((order // TOP_K).astype(jnp.int32))
    slot_gate = jnp.zeros((n_blocks * MOE_BLOCK,), jnp.float32).at[dest].set(gate.reshape(-1)[order])
    block_start = jnp.arange(n_blocks, dtype=jnp.int32) * MOE_BLOCK
    block_e = jnp.minimum(jnp.searchsorted(pad_end, block_start, side='right'), N_EXPERTS - 1)
    x_ext = jnp.concatenate([xt, jnp.zeros((1, d), xt.dtype)], axis=0)
    xb = x_ext[slot_tok].reshape(n_blocks, MOE_BLOCK, d)

    def expert_block(args):
        xblk, e = args
        h = xblk @ w_gu[e] + b_gu[e]
        return clamped_swiglu(h) @ w_down[e] + b_down[e]

    yb = lax.map(expert_block, (xb, block_e)).reshape(-1, d)
    yb = yb * slot_gate[:, None].astype(yb.dtype)
    out = jnp.zeros((n_tok + 1, d), yb.dtype).at[slot_tok].add(yb)[:n_tok]
    return out.reshape(shp)


def setup_inputs(seed: int = 0) -> dict:
    key = jax.random.key(seed)
    ks = list(jax.random.split(key, 32))

    def nrm(i, shape, scale):
        return jax.random.normal(ks[i], shape, jnp.float32) * scale

    d = D_MODEL
    return {
        'x_prompt': nrm(0, (BATCH, SEQ, d), 1.0),
        'x_sample': nrm(1, (DEC_BATCH, DEC_SEQ, d), 1.0),
        'cache_mla_ckv': nrm(2, (N_B_LAYERS, DEC_BATCH, PAST_LEN, KV_LORA), 1.0),
        'cache_mla_krope': nrm(3, (N_B_LAYERS, DEC_BATCH, PAST_LEN, ROPE_DIM), 1.0),
        'ln_mix_g': 1.0 + nrm(4, (DEPTH, d), 0.02),
        'ln_mix_b': nrm(5, (DEPTH, d), 0.02),
        'ln_ffn_g': 1.0 + nrm(6, (DEPTH, d), 0.02),
        'ln_ffn_b': nrm(7, (DEPTH, d), 0.02),
        'a_w_in': nrm(8, (N_A_LAYERS, d, 2 * GMLP_WIDTH), d ** -0.5),
        'a_ln_v_g': 1.0 + nrm(9, (N_A_LAYERS, GMLP_WIDTH), 0.02),
        'a_ln_v_b': nrm(10, (N_A_LAYERS, GMLP_WIDTH), 0.02),
        'a_w_s': nrm(11, (N_A_LAYERS, GMLP_GROUPS, GMLP_CHUNK, GMLP_CHUNK), GMLP_CHUNK ** -0.5),
        'a_b_s': 1.0 + nrm(12, (N_A_LAYERS, GMLP_GROUPS, GMLP_CHUNK), 0.02),
        'a_w_out': nrm(13, (N_A_LAYERS, GMLP_WIDTH, d), BETA * GMLP_WIDTH ** -0.5),
        'b_w_in': nrm(14, (N_B_LAYERS, d, Q_LORA + KV_LORA + ROPE_DIM), d ** -0.5),
        'b_q_norm_g': 1.0 + nrm(15, (N_B_LAYERS, Q_LORA), 0.02),
        'b_kv_norm_g': 1.0 + nrm(16, (N_B_LAYERS, KV_LORA), 0.02),
        'b_w_uq': nrm(17, (N_B_LAYERS, Q_LORA, MLA_HEADS, NOPE_DIM + ROPE_DIM), Q_LORA ** -0.5),
        'b_w_uk': nrm(18, (N_B_LAYERS, KV_LORA, MLA_HEADS, NOPE_DIM), KV_LORA ** -0.5),
        'b_w_uv': nrm(19, (N_B_LAYERS, KV_LORA, MLA_HEADS, V_DIM), BETA * KV_LORA ** -0.5),
        'b_w_out': nrm(20, (N_B_LAYERS, MLA_HEADS * V_DIM, d), BETA * (MLA_HEADS * V_DIM) ** -0.5),
        'router_w': nrm(21, (DEPTH, d, N_EXPERTS), d ** -0.5),
        'router_b': nrm(22, (DEPTH, N_EXPERTS), 0.01),
        'exp_w_gu': nrm(23, (DEPTH, N_EXPERTS, d, 2 * EXPERT_FF), d ** -0.5),
        'exp_b_gu': nrm(24, (DEPTH, N_EXPERTS, 2 * EXPERT_FF), 0.01),
        'exp_w_down': nrm(25, (DEPTH, N_EXPERTS, EXPERT_FF, d), BETA * EXPERT_FF ** -0.5),
        'exp_b_down': nrm(26, (DEPTH, N_EXPERTS, d), 0.01),
    }


def reference(x_prompt, x_sample, cache_mla_ckv, cache_mla_krope,
              ln_mix_g, ln_mix_b, ln_ffn_g, ln_ffn_b,
              a_w_in, a_ln_v_g, a_ln_v_b, a_w_s, a_b_s, a_w_out,
              b_w_in, b_q_norm_g, b_kv_norm_g, b_w_uq, b_w_uk, b_w_uv, b_w_out,
              router_w, router_b, exp_w_gu, exp_b_gu, exp_w_down, exp_b_down):
    p_pos = jnp.arange(x_prompt.shape[1], dtype=jnp.int32)
    past = cache_mla_ckv.shape[2]
    s_len = x_sample.shape[1]
    s_pos = past + jnp.arange(s_len, dtype=jnp.int32)
    k_pos_sample = jnp.arange(past + s_len, dtype=jnp.int32)

    yp, ys = x_prompt, x_sample
    v_s_list, ckv_p_list, kr_p_list, ckv_s_list, kr_s_list = [], [], [], [], []
    for i in range(DEPTH):
        j = i // N_MIXERS
        if i % N_MIXERS == 0:
            mp, _ = gmlp_mixer(yp, a_w_in[j], a_ln_v_g[j], a_ln_v_b[j], a_w_s[j], a_b_s[j], a_w_out[j])
            ms, v_s = gmlp_mixer(ys, a_w_in[j], a_ln_v_g[j], a_ln_v_b[j], a_w_s[j], a_b_s[j], a_w_out[j])
            v_s_list.append(v_s)
        else:
            qn, qr, ckv_p, kr_p = mla_project(yp, p_pos, b_w_in[j], b_q_norm_g[j], b_kv_norm_g[j], b_w_uq[j])
            mp = mla_prompt_attention(qn, qr, ckv_p, kr_p, p_pos, b_w_uk[j], b_w_uv[j]) @ b_w_out[j]
            qn, qr, ckv_s, kr_s = mla_project(ys, s_pos, b_w_in[j], b_q_norm_g[j], b_kv_norm_g[j], b_w_uq[j])
            ckv_all = jnp.concatenate([cache_mla_ckv[j], ckv_s], axis=1)
            kr_all = jnp.concatenate([cache_mla_krope[j], kr_s], axis=1)
            ms = mla_sample_attention(qn, qr, s_pos, ckv_all, kr_all, k_pos_sample, b_w_uk[j], b_w_uv[j]) @ b_w_out[j]
            ckv_p_list.append(ckv_p)
            kr_p_list.append(kr_p)
            ckv_s_list.append(ckv_s)
            kr_s_list.append(kr_s)
        yp = layer_norm(ALPHA * yp + mp, ln_mix_g[i], ln_mix_b[i])
        ys = layer_norm(ALPHA * ys + ms, ln_mix_g[i], ln_mix_b[i])
        yp = layer_norm(ALPHA * yp + moe_ffn(yp, router_w[i], router_b[i], exp_w_gu[i], exp_b_gu[i], exp_w_down[i], exp_b_down[i]), ln_ffn_g[i], ln_ffn_b[i])
        ys = layer_norm(ALPHA * ys + moe_ffn(ys, router_w[i], router_b[i], exp_w_gu[i], exp_b_gu[i], exp_w_down[i], exp_b_down[i]), ln_ffn_g[i], ln_ffn_b[i])

    new_gmlp_v_sample = jnp.stack(v_s_list)
    new_ckv_prompt = jnp.stack(ckv_p_list)
    new_krope_prompt = jnp.stack(kr_p_list)
    new_ckv_sample = jnp.stack(ckv_s_list)
    new_krope_sample = jnp.stack(kr_s_list)
    return (yp, ys, new_gmlp_v_sample, new_ckv_prompt, new_krope_prompt, new_ckv_sample, new_krope_sample)
```

```python
import functools

import numpy as np
import jax
import jax.numpy as jnp
from jax import lax
from jax.experimental import pallas as pl
from jax.experimental.pallas import tpu as pltpu

F32 = jnp.float32
BF16 = jnp.bfloat16

CHUNK = 64
GMLP_CHUNK = 128
GMLP_GROUPS = 8
TOP_K = 4
ROPE_THETA = 10000.0
NEG_INF = -1e30
SWIGLU_LIMIT = 7.0
SWIGLU_ALPHA = 1.702
LN_EPS = 1e-5
RMS_EPS = 1e-6

LANES = 128
MOE_SUPER_ROWS = 1024
MOE_SUB_ROWS = 256
VMEM_LIMIT = 56 * 1024 * 1024


def _tile(dim, pref):
    if dim <= pref:
        return dim
    t = pref
    while dim % t:
        t //= 2
    return t


def _params(*sem):
    return pltpu.CompilerParams(dimension_semantics=sem, vmem_limit_bytes=VMEM_LIMIT)


def _layer_norm(x, g, b):
    mu = jnp.mean(x, axis=-1, keepdims=True)
    xc = x - mu
    var = jnp.mean(xc * xc, axis=-1, keepdims=True)
    return xc * lax.rsqrt(var + LN_EPS) * g + b


def _rms_norm(x, g):
    return x * lax.rsqrt(jnp.mean(x * x, axis=-1, keepdims=True) + RMS_EPS) * g


def _gelu(x):
    return 0.5 * x * (1.0 + lax.erf(x * np.float32(1.0 / np.sqrt(2.0))))


def _swap_rope_halves(x):
    half = 32
    lane = lax.broadcasted_iota(jnp.int32, x.shape, 1)
    fwd = pltpu.roll(x, LANES - half, 1)
    bwd = pltpu.roll(x, half, 1)
    return jnp.where((lane % (2 * half)) < half, fwd, bwd)


def _mm_kernel(*refs, alpha, has_res):
    if has_res:
        x_ref, w_ref, r_ref, o_ref = refs
    else:
        x_ref, w_ref, o_ref = refs
    acc = jnp.dot(x_ref[...], w_ref[...].astype(BF16), preferred_element_type=F32)
    if has_res:
        acc = alpha * r_ref[...] + acc
    o_ref[...] = acc.astype(o_ref.dtype)


def _matmul(x, w, *, out_dtype, res=None, alpha=1.0, tm=1024, tn=512, name):
    m, k = x.shape
    n = w.shape[1]
    tm, tn = _tile(m, tm), _tile(n, tn)
    in_specs = [pl.BlockSpec((tm, k), lambda i, j: (i, 0)),
                pl.BlockSpec((k, tn), lambda i, j: (0, j))]
    args = [x, w]
    if res is not None:
        in_specs.append(pl.BlockSpec((tm, tn), lambda i, j: (i, j)))
        args.append(res)
    return pl.pallas_call(
        functools.partial(_mm_kernel, alpha=alpha, has_res=res is not None),
        grid=(m // tm, n // tn),
        in_specs=in_specs,
        out_specs=pl.BlockSpec((tm, tn), lambda i, j: (i, j)),
        out_shape=jax.ShapeDtypeStruct((m, n), out_dtype),
        compiler_params=_params("parallel", "arbitrary"),
        name=name,
    )(*args)


def _gmlp_in_kernel(x_ref, wu_ref, wv_ref, u_ref, v_ref):
    x = x_ref[...]
    u_ref[...] = _gelu(jnp.dot(x, wu_ref[...].astype(BF16), preferred_element_type=F32))
    v_ref[...] = _gelu(jnp.dot(x, wv_ref[...].astype(BF16), preferred_element_type=F32))


def _gmlp_in(xb, w_in):
    m, k = xb.shape
    width = w_in.shape[1] // 2
    tm, tn = _tile(m, 1024), _tile(width, 256)
    nj = width // tn
    return pl.pallas_call(
        _gmlp_in_kernel,
        grid=(m // tm, nj),
        in_specs=[pl.BlockSpec((tm, k), lambda i, j: (i, 0)),
                  pl.BlockSpec((k, tn), lambda i, j: (0, j)),
                  pl.BlockSpec((k, tn), lambda i, j: (0, j + nj))],
        out_specs=[pl.BlockSpec((tm, tn), lambda i, j: (i, j)),
                   pl.BlockSpec((tm, tn), lambda i, j: (i, j))],
        out_shape=[jax.ShapeDtypeStruct((m, width), F32)] * 2,
        compiler_params=_params("parallel", "arbitrary"),
        name="gmlp_in",
    )(xb, w_in, w_in)


def _gmlp_gate_kernel(u_ref, v_ref, g_ref, b_ref, ws_ref, bs_ref, o_ref, vn_ref, *, n_prompt_tiles,
                      sample_shift):
    i = pl.program_id(0)
    rows = u_ref.shape[0]
    gdim = u_ref.shape[1] // GMLP_GROUPS
    vn = _layer_norm(v_ref[...], g_ref[...], b_ref[...])
    vn_ref[...] = vn
    vnb = vn.astype(BF16)
    shift = jnp.where(i >= n_prompt_tiles, sample_shift, int(np.log2(GMLP_CHUNK)))
    r = lax.broadcasted_iota(jnp.int32, (rows, rows), 0)
    c = lax.broadcasted_iota(jnp.int32, (rows, rows), 1)
    keep = (c <= r) & ((r >> shift) == (c >> shift))
    bias = bs_ref[...]
    for g in range(GMLP_GROUPS):
        w = jnp.where(keep, ws_ref[g], 0.0).astype(BF16)
        s = jnp.dot(w, vnb[:, g * gdim:(g + 1) * gdim], preferred_element_type=F32)
        s = s + bias[:, g:g + 1]
        o_ref[:, g * gdim:(g + 1) * gdim] = (u_ref[:, g * gdim:(g + 1) * gdim] * s).astype(o_ref.dtype)


def _gmlp_gate(u, v, ln_g, ln_b, w_s, b_s, n_prompt, dec_seq):
    t, width = u.shape
    rows = GMLP_CHUNK
    reps = rows // dec_seq
    w_prompt = w_s
    w_sample = jnp.tile(w_s[:, :dec_seq, :dec_seq], (1, reps, reps))
    b_prompt = b_s.T
    b_sample = jnp.tile(b_s[:, :dec_seq].T, (reps, 1))
    w_all = jnp.stack([w_prompt, w_sample])
    b_all = jnp.stack([b_prompt, b_sample])
    n_prompt_tiles = n_prompt // rows
    sel = lambda i: (i >= n_prompt_tiles).astype(jnp.int32)
    return pl.pallas_call(
        functools.partial(_gmlp_gate_kernel, n_prompt_tiles=n_prompt_tiles,
                          sample_shift=int(np.log2(dec_seq))),
        grid=(t // rows,),
        in_specs=[pl.BlockSpec((rows, width), lambda i: (i, 0)),
                  pl.BlockSpec((rows, width), lambda i: (i, 0)),
                  pl.BlockSpec((1, width), lambda i: (0, 0)),
                  pl.BlockSpec((1, width), lambda i: (0, 0)),
                  pl.BlockSpec((None, GMLP_GROUPS, rows, rows), lambda i: (sel(i), 0, 0, 0)),
                  pl.BlockSpec((None, rows, GMLP_GROUPS), lambda i: (sel(i), 0, 0))],
        out_specs=[pl.BlockSpec((rows, width), lambda i: (i, 0)),
                   pl.BlockSpec((rows, width), lambda i: (i, 0))],
        out_shape=[jax.ShapeDtypeStruct((t, width), BF16), jax.ShapeDtypeStruct((t, width), F32)],
        compiler_params=_params("parallel"),
        name="gmlp_gate",
    )(u, v, ln_g.reshape(1, -1), ln_b.reshape(1, -1), w_all, b_all)


def _ln_router_kernel(p_ref, g_ref, b_ref, rw_ref, rb_ref, y_ref, idx_ref, gate_ref):
    y = _layer_norm(p_ref[...], g_ref[...], b_ref[...])
    y_ref[...] = y
    logits = jnp.dot(y, rw_ref[...], precision=lax.Precision.HIGHEST,
                     preferred_element_type=F32) + rb_ref[...]
    n_e = logits.shape[1]
    col = lax.broadcasted_iota(jnp.int32, logits.shape, 1)
    vals, idxs = [], []
    cur = logits
    for _ in range(TOP_K):
        mx = jnp.max(cur, axis=-1, keepdims=True)
        am = jnp.min(jnp.where(cur == mx, col, n_e), axis=-1, keepdims=True)
        vals.append(mx)
        idxs.append(am)
        cur = jnp.where(col == am, -jnp.inf, cur)
    es = [jnp.exp(v - vals[0]) for v in vals]
    denom = es[0]
    for e in es[1:]:
        denom = denom + e
    kcol = lax.broadcasted_iota(jnp.int32, (logits.shape[0], TOP_K), 1)
    idx_out = jnp.zeros((logits.shape[0], TOP_K), jnp.int32)
    gate_out = jnp.zeros((logits.shape[0], TOP_K), F32)
    for k in range(TOP_K):
        idx_out = jnp.where(kcol == k, idxs[k], idx_out)
        gate_out = jnp.where(kcol == k, es[k] / denom, gate_out)
    idx_ref[...] = idx_out
    gate_ref[...] = gate_out


def _ln_router(pre, g, b, rw, rb):
    t, d = pre.shape
    n_e = rw.shape[1]
    tm = _tile(t, 256)
    return pl.pallas_call(
        _ln_router_kernel,
        grid=(t // tm,),
        in_specs=[pl.BlockSpec((tm, d), lambda i: (i, 0)),
                  pl.BlockSpec((1, d), lambda i: (0, 0)),
                  pl.BlockSpec((1, d), lambda i: (0, 0)),
                  pl.BlockSpec((d, n_e), lambda i: (0, 0)),
                  pl.BlockSpec((1, n_e), lambda i: (0, 0))],
        out_specs=[pl.BlockSpec((tm, d), lambda i: (i, 0)),
                   pl.BlockSpec((tm, TOP_K), lambda i: (i, 0)),
                   pl.BlockSpec((tm, TOP_K), lambda i: (i, 0))],
        out_shape=[jax.ShapeDtypeStruct((t, d), F32),
                   jax.ShapeDtypeStruct((t, TOP_K), jnp.int32),
                   jax.ShapeDtypeStruct((t, TOP_K), F32)],
        compiler_params=_params("parallel"),
        name="ln_router",
    )(pre, g.reshape(1, -1), b.reshape(1, -1), rw, rb.reshape(1, -1))


def _route_tables(top_idx, n_e, n_sb):
    t = top_idx.shape[0]
    flat_e = top_idx.reshape(-1)
    onehot = (flat_e[:, None] == jnp.arange(n_e, dtype=jnp.int32)[None, :]).astype(jnp.int32)
    csum = jnp.cumsum(onehot, axis=0)
    rank = jnp.take_along_axis(csum, flat_e[:, None], axis=1)[:, 0] - 1
    counts = csum[-1]
    sb_per_e = (counts + MOE_SUPER_ROWS - 1) // MOE_SUPER_ROWS
    sb_end = jnp.cumsum(sb_per_e)
    sb_first = sb_end - sb_per_e
    n_used = sb_end[-1]
    sb = jnp.arange(n_sb, dtype=jnp.int32)
    sb_c = jnp.minimum(sb, n_used - 1)
    e_of = jnp.minimum(jnp.searchsorted(sb_end, sb_c, side="right"), n_e - 1).astype(jnp.int32)
    rows = jnp.clip(counts[e_of] - (sb_c - sb_first[e_of]) * MOE_SUPER_ROWS, 0, MOE_SUPER_ROWS)
    rows = jnp.where(sb < n_used, rows, 0).astype(jnp.int32)
    dest = (sb_first[flat_e] * MOE_SUPER_ROWS + rank).astype(jnp.int32)
    tok = (jnp.arange(t * TOP_K, dtype=jnp.int32) // TOP_K)
    slot_tok = jnp.zeros((n_sb * MOE_SUPER_ROWS,), jnp.int32).at[dest].set(tok)
    return sb_c.astype(jnp.int32), e_of, rows, slot_tok, dest


def _moe_up_kernel(sb_blk, sb_e, sb_rows, slot_tok, y_hbm, wg_ref, wu_ref, bg_ref, bu_ref, act_ref,
                   xf32, xb16, wb16, sem):
    sb = pl.program_id(0)
    j = pl.program_id(1)
    n_sb = pl.num_programs(0)
    tn = wg_ref.shape[1]
    rows = sb_rows[sb]

    def row_copy(tok, r):
        return pltpu.make_async_copy(y_hbm.at[pl.ds(tok, 1)], xf32.at[pl.ds(r, 1)], sem.at[0])

    def issue(sb_i):
        base = sb_i * MOE_SUPER_ROWS

        def body(r, c):
            row_copy(slot_tok[base + r], r).start()
            return c
        lax.fori_loop(0, sb_rows[sb_i], body, 0)

    @pl.when((sb == 0) & (j == 0))
    def _():
        xf32[...] = jnp.zeros_like(xf32)
        issue(0)

    @pl.when(j == 0)
    def _():
        def wait_body(r, c):
            row_copy(0, r).wait()
            return c
        lax.fori_loop(0, rows, wait_body, 0)
        xb16[...] = xf32[...].astype(BF16)

        @pl.when(sb + 1 < n_sb)
        def _():
            issue(sb + 1)

    @pl.when(rows > 0)
    def _():
        wb16[:, :tn] = wg_ref[...].astype(BF16)
        wb16[:, tn:] = wu_ref[...].astype(BF16)
        bg = bg_ref[...]
        bu = bu_ref[...]
        n_sub = (rows + MOE_SUB_ROWS - 1) // MOE_SUB_ROWS

        def sub_body(s, c):
            r0 = pl.multiple_of(s * MOE_SUB_ROWS, MOE_SUB_ROWS)
            h = jnp.dot(xb16[pl.ds(r0, MOE_SUB_ROWS), :], wb16[...], preferred_element_type=F32)
            gate = jnp.minimum(h[:, :tn] + bg, SWIGLU_LIMIT)
            up = jnp.clip(h[:, tn:] + bu, -SWIGLU_LIMIT, SWIGLU_LIMIT)
            a = (up + 1.0) * gate * jax.nn.sigmoid(SWIGLU_ALPHA * gate)
            act_ref[pl.ds(r0, MOE_SUB_ROWS), :] = a.astype(act_ref.dtype)
            return c
        lax.fori_loop(0, n_sub, sub_body, 0)

        def zero_body(s, c):
            r0 = pl.multiple_of(s * MOE_SUB_ROWS, MOE_SUB_ROWS)
            act_ref[pl.ds(r0, MOE_SUB_ROWS), :] = jnp.zeros((MOE_SUB_ROWS, tn), act_ref.dtype)
            return c
        lax.fori_loop(n_sub, MOE_SUPER_ROWS // MOE_SUB_ROWS, zero_body, 0)


def _moe_up(y, w_gu, b_gu, sb_blk, sb_e, sb_rows, slot_tok):
    t, d = y.shape
    n_e, _, ff2 = w_gu.shape
    ff = ff2 // 2
    n_sb = sb_e.shape[0]
    tn = _tile(ff, 256)
    nj = ff // tn
    b3 = b_gu.reshape(n_e, 1, ff2)

    def jj(sb, j, rows):
        return jnp.where(rows[sb] > 0, j, nj - 1)

    grid_spec = pltpu.PrefetchScalarGridSpec(
        num_scalar_prefetch=4,
        grid=(n_sb, nj),
        in_specs=[pl.BlockSpec(memory_space=pl.ANY),
                  pl.BlockSpec((None, d, tn), lambda sb, j, blk, e, rows, st: (e[sb], 0, jj(sb, j, rows))),
                  pl.BlockSpec((None, d, tn), lambda sb, j, blk, e, rows, st: (e[sb], 0, jj(sb, j, rows) + nj)),
                  pl.BlockSpec((None, 1, tn), lambda sb, j, blk, e, rows, st: (e[sb], 0, jj(sb, j, rows))),
                  pl.BlockSpec((None, 1, tn), lambda sb, j, blk, e, rows, st: (e[sb], 0, jj(sb, j, rows) + nj))],
        out_specs=pl.BlockSpec((MOE_SUPER_ROWS, tn),
                               lambda sb, j, blk, e, rows, st: (blk[sb], jj(sb, j, rows))),
        scratch_shapes=[pltpu.VMEM((MOE_SUPER_ROWS, d), F32),
                        pltpu.VMEM((MOE_SUPER_ROWS, d), BF16),
                        pltpu.VMEM((d, 2 * tn), BF16),
                        pltpu.SemaphoreType.DMA((1,))],
    )
    return pl.pallas_call(
        _moe_up_kernel,
        grid_spec=grid_spec,
        out_shape=jax.ShapeDtypeStruct((n_sb * MOE_SUPER_ROWS, ff), BF16),
        compiler_params=_params("arbitrary", "arbitrary"),
        name="moe_up",
    )(sb_blk, sb_e, sb_rows, slot_tok, y, w_gu, w_gu, b3, b3)


def _moe_down_kernel(sb_blk, sb_e, sb_rows, a_ref, w_ref, b_ref, o_ref, wb16):
    sb = pl.program_id(0)
    tn = w_ref.shape[1]
    rows = sb_rows[sb]
    @pl.when(rows > 0)
    def _():
        wb16[...] = w_ref[...].astype(BF16)
        bias = b_ref[...]
        n_sub = (rows + MOE_SUB_ROWS - 1) // MOE_SUB_ROWS

        def sub_body(s, c):
            r0 = pl.multiple_of(s * MOE_SUB_ROWS, MOE_SUB_ROWS)
            o_ref[pl.ds(r0, MOE_SUB_ROWS), :] = jnp.dot(
                a_ref[pl.ds(r0, MOE_SUB_ROWS), :], wb16[...], preferred_element_type=F32) + bias
            return c
        lax.fori_loop(0, n_sub, sub_body, 0)

        def zero_body(s, c):
            r0 = pl.multiple_of(s * MOE_SUB_ROWS, MOE_SUB_ROWS)
            o_ref[pl.ds(r0, MOE_SUB_ROWS), :] = jnp.zeros((MOE_SUB_ROWS, tn), F32)
            return c
        lax.fori_loop(n_sub, MOE_SUPER_ROWS // MOE_SUB_ROWS, zero_body, 0)


def _moe_down(act, w_down, b_down, sb_blk, sb_e, sb_rows):
    n_slots, ff = act.shape
    n_e, _, d = w_down.shape
    n_sb = sb_e.shape[0]
    tn = _tile(d, 512)
    nj = d // tn
    b3 = b_down.reshape(n_e, 1, d)

    def jj(sb, j, rows):
        return jnp.where(rows[sb] > 0, j, nj - 1)

    grid_spec = pltpu.PrefetchScalarGridSpec(
        num_scalar_prefetch=3,
        grid=(n_sb, nj),
        in_specs=[pl.BlockSpec((MOE_SUPER_ROWS, ff), lambda sb, j, blk, e, rows: (blk[sb], 0)),
                  pl.BlockSpec((None, ff, tn), lambda sb, j, blk, e, rows: (e[sb], 0, jj(sb, j, rows))),
                  pl.BlockSpec((None, 1, tn), lambda sb, j, blk, e, rows: (e[sb], 0, jj(sb, j, rows)))],
        out_specs=pl.BlockSpec((MOE_SUPER_ROWS, tn), lambda sb, j, blk, e, rows: (blk[sb], jj(sb, j, rows))),
        scratch_shapes=[pltpu.VMEM((ff, tn), BF16)],
    )
    return pl.pallas_call(
        _moe_down_kernel,
        grid_spec=grid_spec,
        out_shape=jax.ShapeDtypeStruct((n_slots, d), F32),
        compiler_params=_params("arbitrary", "arbitrary"),
        name="moe_down",
    )(sb_blk, sb_e, sb_rows, act, w_down, b3)


def _moe_combine_kernel(dest, ys_hbm, gate_ref, y_ref, g_ref, b_ref, o_ref, ob_ref, gbuf, sem, *, alpha):
    i = pl.program_id(0)
    n = pl.num_programs(0)
    tm = y_ref.shape[0]
    slot = i % 2

    def row_copy(src_row, k, r, s):
        return pltpu.make_async_copy(ys_hbm.at[pl.ds(src_row, 1)], gbuf.at[s, k, pl.ds(r, 1)], sem.at[s])

    def issue(tile, s):
        base = tile * tm * TOP_K

        def body(r, c):
            for k in range(TOP_K):
                row_copy(dest[base + r * TOP_K + k], k, r, s).start()
            return c
        lax.fori_loop(0, tm, body, 0)

    @pl.when(i == 0)
    def _():
        issue(0, 0)

    @pl.when(i + 1 < n)
    def _():
        issue(i + 1, 1 - slot)

    def wait_body(r, c):
        for k in range(TOP_K):
            row_copy(0, k, r, slot).wait()
        return c
    lax.fori_loop(0, tm, wait_body, 0)

    gates = gate_ref[...]
    moe = gates[:, 0:1] * gbuf[slot, 0]
    for k in range(1, TOP_K):
        moe = moe + gates[:, k:k + 1] * gbuf[slot, k]
    out = _layer_norm(alpha * y_ref[...] + moe, g_ref[...], b_ref[...])
    o_ref[...] = out
    ob_ref[...] = out.astype(ob_ref.dtype)


def _moe_combine(ys, dest, gates, y, g, b, alpha):
    t, d = y.shape
    tm = _tile(t, 128)
    grid_spec = pltpu.PrefetchScalarGridSpec(
        num_scalar_prefetch=1,
        grid=(t // tm,),
        in_specs=[pl.BlockSpec(memory_space=pl.ANY),
                  pl.BlockSpec((tm, TOP_K), lambda i, dst: (i, 0)),
                  pl.BlockSpec((tm, d), lambda i, dst: (i, 0)),
                  pl.BlockSpec((1, d), lambda i, dst: (0, 0)),
                  pl.BlockSpec((1, d), lambda i, dst: (0, 0))],
        out_specs=[pl.BlockSpec((tm, d), lambda i, dst: (i, 0)),
                   pl.BlockSpec((tm, d), lambda i, dst: (i, 0))],
        scratch_shapes=[pltpu.VMEM((2, TOP_K, tm, d), F32),
                        pltpu.SemaphoreType.DMA((2,))],
    )
    return pl.pallas_call(
        functools.partial(_moe_combine_kernel, alpha=alpha),
        grid_spec=grid_spec,
        out_shape=[jax.ShapeDtypeStruct((t, d), F32), jax.ShapeDtypeStruct((t, d), BF16)],
        compiler_params=_params("arbitrary"),
        name="moe_combine",
    )(dest, ys, gates, y, g.reshape(1, -1), b.reshape(1, -1))


def _moe_layer(pre, ln_g, ln_b, rw, rb, w_gu, b_gu, w_down, b_down, ffn_g, ffn_b, alpha):
    t = pre.shape[0]
    n_e = rw.shape[1]
    y, top_idx, gates = _ln_router(pre, ln_g, ln_b, rw, rb)
    n_sb = (t * TOP_K) // MOE_SUPER_ROWS + n_e
    sb_blk, sb_e, sb_rows, slot_tok, dest = _route_tables(top_idx, n_e, n_sb)
    act = _moe_up(y, w_gu, b_gu, sb_blk, sb_e, sb_rows, slot_tok)
    ys = _moe_down(act, w_down, b_down, sb_blk, sb_e, sb_rows)
    return _moe_combine(ys, dest, gates, y, ffn_g, ffn_b, alpha)


def _mla_latent_kernel(h_ref, qg_ref, kvg_ref, cos_ref, sin_ref, cq_ref, ckv_ref, ckvb_ref, kr_ref,
                       krp_ref, *, q_lora, kv_lora, rope_dim):
    h = h_ref[...]
    cq_ref[...] = _rms_norm(h[:, :q_lora], qg_ref[...]).astype(cq_ref.dtype)
    ckv = _rms_norm(h[:, q_lora:q_lora + kv_lora], kvg_ref[...])
    ckv_ref[...] = ckv
    ckvb_ref[...] = ckv.astype(ckvb_ref.dtype)
    x = h[:, q_lora + kv_lora:q_lora + kv_lora + LANES]
    kr = x * cos_ref[...] + _swap_rope_halves(x) * sin_ref[...]
    kr_ref[...] = kr[:, :rope_dim]
    krp_ref[...] = kr.astype(krp_ref.dtype)


def _mla_latent(h, q_g, kv_g, cos_t, sin_t, q_lora, kv_lora, rope_dim):
    t, hw = h.shape
    tm = _tile(t, 512)
    return pl.pallas_call(
        functools.partial(_mla_latent_kernel, q_lora=q_lora, kv_lora=kv_lora, rope_dim=rope_dim),
        grid=(t // tm,),
        in_specs=[pl.BlockSpec((tm, hw), lambda i: (i, 0)),
                  pl.BlockSpec((1, q_lora), lambda i: (0, 0)),
                  pl.BlockSpec((1, kv_lora), lambda i: (0, 0)),
                  pl.BlockSpec((tm, LANES), lambda i: (i, 0)),
                  pl.BlockSpec((tm, LANES), lambda i: (i, 0))],
        out_specs=[pl.BlockSpec((tm, q_lora), lambda i: (i, 0)),
                   pl.BlockSpec((tm, kv_lora), lambda i: (i, 0)),
                   pl.BlockSpec((tm, kv_lora), lambda i: (i, 0)),
                   pl.BlockSpec((tm, rope_dim), lambda i: (i, 0)),
                   pl.BlockSpec((tm, LANES), lambda i: (i, 0))],
        out_shape=[jax.ShapeDtypeStruct((t, q_lora), BF16),
                   jax.ShapeDtypeStruct((t, kv_lora), F32),
                   jax.ShapeDtypeStruct((t, kv_lora), BF16),
                   jax.ShapeDtypeStruct((t, rope_dim), F32),
                   jax.ShapeDtypeStruct((t, LANES), BF16)],
        compiler_params=_params("parallel"),
        name="mla_latent",
    )(h, q_g.reshape(1, -1), kv_g.reshape(1, -1), cos_t, sin_t)


def _q_proj_kernel(x_ref, w_ref, cos_ref, sin_ref, o_ref):
    acc = jnp.dot(x_ref[...], w_ref[...].astype(BF16), preferred_element_type=F32)
    cos, sin = cos_ref[...], sin_ref[...]
    for g in range(acc.shape[1] // LANES):
        blk = acc[:, g * LANES:(g + 1) * LANES]
        if g % 2:
            blk = blk * cos + _swap_rope_halves(blk) * sin
        o_ref[:, g * LANES:(g + 1) * LANES] = blk.astype(o_ref.dtype)


def _q_proj(cq, w_q, cos_t, sin_t):
    t, k = cq.shape
    n = w_q.shape[1]
    tm, tn = _tile(t, 1024), _tile(n, 512)
    return pl.pallas_call(
        _q_proj_kernel,
        grid=(t // tm, n // tn),
        in_specs=[pl.BlockSpec((tm, k), lambda i, j: (i, 0)),
                  pl.BlockSpec((k, tn), lambda i, j: (0, j)),
                  pl.BlockSpec((tm, LANES), lambda i, j: (i, 0)),
                  pl.BlockSpec((tm, LANES), lambda i, j: (i, 0))],
        out_specs=pl.BlockSpec((tm, tn), lambda i, j: (i, j)),
        out_shape=jax.ShapeDtypeStruct((t, n), BF16),
        compiler_params=_params("parallel", "arbitrary"),
        name="q_proj",
    )(cq, w_q, cos_t, sin_t)


def _prompt_attn_kernel(q_ref, kn_ref, kr_ref, v_ref, o_ref, kcat, m_ref, l_ref, acc_ref, *, scale, tq):
    qi = pl.program_id(1)
    hd = 2 * LANES
    vd = LANES

    @pl.when(qi == 0)
    def _():
        for a in range(2):
            kcat[a, :, :LANES] = kn_ref[:, a * LANES:(a + 1) * LANES]
            kcat[a, :, LANES:] = kr_ref[...]

    r = lax.broadcasted_iota(jnp.int32, (tq, tq), 0)
    c = lax.broadcasted_iota(jnp.int32, (tq, tq), 1)
    diag_ok = (c // CHUNK) <= (r // CHUNK)

    for a in range(2):
        q = q_ref[:, a * hd:(a + 1) * hd]
        m_ref[...] = jnp.full(m_ref.shape, NEG_INF, F32)
        l_ref[...] = jnp.zeros(l_ref.shape, F32)
        acc_ref[...] = jnp.zeros(acc_ref.shape, F32)

        def kv_step(kb, masked):
            k0 = pl.multiple_of(kb * tq, tq)
            k = kcat[a, pl.ds(k0, tq), :]
            s = lax.dot_general(q, k, (((1,), (1,)), ((), ())), preferred_element_type=F32) * scale
            if masked:
                s = jnp.where(diag_ok, s, NEG_INF)
            m_prev = m_ref[...]
            m_new = jnp.maximum(m_prev, jnp.max(s, axis=-1, keepdims=True))
            corr = jnp.exp(m_prev - m_new)
            p = jnp.exp(s - m_new)
            l_ref[...] = corr * l_ref[...] + jnp.sum(p, axis=-1, keepdims=True)
            acc_ref[...] = corr * acc_ref[...] + jnp.dot(
                p.astype(BF16), v_ref[pl.ds(k0, tq), a * vd:(a + 1) * vd], preferred_element_type=F32)
            m_ref[...] = m_new

        def body(kb, carry):
            kv_step(kb, False)
            return carry
        lax.fori_loop(0, qi, body, 0)
        kv_step(qi, True)
        o_ref[:, a * vd:(a + 1) * vd] = (acc_ref[...] / l_ref[...]).astype(o_ref.dtype)


def _prompt_attn(q, kn, krp, v, n_prompt, n_heads, scale):
    tq = _tile(n_prompt, 512)
    return pl.pallas_call(
        functools.partial(_prompt_attn_kernel, scale=scale, tq=tq),
        grid=(n_heads // 2, n_prompt // tq),
        in_specs=[pl.BlockSpec((tq, 4 * LANES), lambda p, i: (i, p)),
                  pl.BlockSpec((n_prompt, 2 * LANES), lambda p, i: (0, p)),
                  pl.BlockSpec((n_prompt, LANES), lambda p, i: (0, 0)),
                  pl.BlockSpec((n_prompt, 2 * LANES), lambda p, i: (0, p))],
        out_specs=pl.BlockSpec((tq, 2 * LANES), lambda p, i: (i, p)),
        out_shape=jax.ShapeDtypeStruct((n_prompt, n_heads * LANES), BF16),
        scratch_shapes=[pltpu.VMEM((2, n_prompt, 2 * LANES), BF16),
                        pltpu.VMEM((tq, 1), F32),
                        pltpu.VMEM((tq, 1), F32),
                        pltpu.VMEM((tq, LANES), F32)],
        compiler_params=_params("parallel", "arbitrary"),
        name="prompt_attn",
    )(q, kn, krp, v)


def _q_latent_kernel(q_ref, wuk_ref, o_ref, *, kv_lora):
    qn = q_ref[:, :LANES]
    q_lat = lax.dot_general(qn, wuk_ref[...].astype(BF16), (((1,), (1,)), ((), ())),
                            preferred_element_type=F32)
    o_ref[:, :kv_lora] = q_lat.astype(o_ref.dtype)
    o_ref[:, kv_lora:] = q_ref[:, LANES:]


def _q_latent(q, w_uk_h, n_prompt):
    n_heads, kv_lora, _ = w_uk_h.shape
    n_s = q.shape[0] - n_prompt
    assert n_prompt % n_s == 0
    rb = n_prompt // n_s
    return pl.pallas_call(
        functools.partial(_q_latent_kernel, kv_lora=kv_lora),
        grid=(n_heads,),
        in_specs=[pl.BlockSpec((n_s, 2 * LANES), lambda h: (rb, h)),
                  pl.BlockSpec((None, kv_lora, LANES), lambda h: (h, 0, 0))],
        out_specs=pl.BlockSpec((None, n_s, kv_lora + LANES), lambda h: (h, 0, 0)),
        out_shape=jax.ShapeDtypeStruct((n_heads, n_s, kv_lora + LANES), BF16),
        compiler_params=_params("parallel"),
        name="q_latent",
    )(q, w_uk_h)


def _sample_attn_kernel(q_ref, cc_ref, ck_ref, nc_ref, nk_ref, o_ref, kcat, *, scale, past, dec_seq,
                        kv_lora, n_keys_pad):
    n_heads = q_ref.shape[0]
    n_q = n_heads * dec_seq
    kw = kv_lora + LANES
    kcat[0:past, 0:kv_lora] = cc_ref[...].astype(BF16)
    kcat[0:past, kv_lora:kw] = ck_ref[...].astype(BF16)
    kcat[past:past + dec_seq, 0:kv_lora] = nc_ref[...]
    kcat[past:past + dec_seq, kv_lora:kw] = nk_ref[...]
    kcat[past + dec_seq:n_keys_pad, :] = jnp.zeros((n_keys_pad - past - dec_seq, kw), BF16)
    q = q_ref[...].reshape(n_q, kw)
    s = lax.dot_general(q, kcat[...], (((1,), (1,)), ((), ())), preferred_element_type=F32) * scale
    q_pos = past + lax.broadcasted_iota(jnp.int32, s.shape, 0) % dec_seq
    k_pos = lax.broadcasted_iota(jnp.int32, s.shape, 1)
    ok = ((k_pos // CHUNK) <= (q_pos // CHUNK)) & (k_pos < past + dec_seq)
    s = jnp.where(ok, s, NEG_INF)
    p = jnp.exp(s - jnp.max(s, axis=-1, keepdims=True))
    p = p / jnp.sum(p, axis=-1, keepdims=True)
    o = jnp.dot(p.astype(BF16), kcat[:, 0:kv_lora], preferred_element_type=F32)
    o_ref[...] = o.astype(o_ref.dtype).reshape(n_heads, dec_seq, kv_lora)


def _sample_attn(q_lat, cache_ckv, cache_krp, ckv_b, krp_b, row0, scale, dec_seq):
    n_heads, n_s, kw = q_lat.shape
    n_b, past, kv_lora = cache_ckv.shape
    n_keys_pad = -(-(past + dec_seq) // LANES) * LANES
    rb0 = row0 // dec_seq
    return pl.pallas_call(
        functools.partial(_sample_attn_kernel, scale=scale, past=past, dec_seq=dec_seq, kv_lora=kv_lora,
                          n_keys_pad=n_keys_pad),
        grid=(n_b,),
        in_specs=[pl.BlockSpec((n_heads, dec_seq, kw), lambda b: (0, b, 0)),
                  pl.BlockSpec((None, past, kv_lora), lambda b: (b, 0, 0)),
                  pl.BlockSpec((None, past, LANES), lambda b: (b, 0, 0)),
                  pl.BlockSpec((dec_seq, kv_lora), lambda b: (rb0 + b, 0)),
                  pl.BlockSpec((dec_seq, LANES), lambda b: (rb0 + b, 0))],
        out_specs=pl.BlockSpec((n_heads, dec_seq, kv_lora), lambda b: (0, b, 0)),
        out_shape=jax.ShapeDtypeStruct((n_heads, n_s, kv_lora), BF16),
        scratch_shapes=[pltpu.VMEM((n_keys_pad, kw), BF16)],
        compiler_params=_params("parallel"),
        name="sample_attn",
    )(q_lat, cache_ckv, cache_krp, ckv_b, krp_b)


def _o_latent_kernel(o_ref, wuv_ref, out_ref):
    out_ref[...] = jnp.dot(o_ref[...], wuv_ref[...].astype(BF16),
                           preferred_element_type=F32).astype(out_ref.dtype)


def _o_latent(o_lat, w_uv_h):
    n_heads, n_s, kv_lora = o_lat.shape
    return pl.pallas_call(
        _o_latent_kernel,
        grid=(n_heads,),
        in_specs=[pl.BlockSpec((None, n_s, kv_lora), lambda h: (h, 0, 0)),
                  pl.BlockSpec((None, kv_lora, LANES), lambda h: (h, 0, 0))],
        out_specs=pl.BlockSpec((n_s, LANES), lambda h: (0, h)),
        out_shape=jax.ShapeDtypeStruct((n_s, n_heads * LANES), BF16),
        compiler_params=_params("parallel"),
        name="o_latent",
    )(o_lat, w_uv_h)


def _rope_tables(pos, rope_dim):
    half = rope_dim // 2
    inv_freq = ROPE_THETA ** (-jnp.arange(half, dtype=F32) / half)
    ang = pos.astype(F32)[:, None] * inv_freq[None, :]
    cos, sin = jnp.cos(ang), jnp.sin(ang)
    zeros = jnp.zeros((pos.shape[0], LANES - rope_dim), F32)
    return (jnp.concatenate([cos, cos, zeros], axis=1), jnp.concatenate([-sin, sin, zeros], axis=1))


def _mla_mixer(xb, n_prompt, dec_seq, cache_ckv, cache_krope, w_in, q_g, kv_g, w_uq, w_uk, w_uv):
    t = xb.shape[0]
    q_lora = q_g.shape[0]
    kv_lora, n_heads, nope = w_uk.shape
    rope_dim = w_uq.shape[2] - nope
    v_dim = w_uv.shape[2]
    assert nope == LANES and v_dim == LANES and rope_dim * 2 == LANES
    past = cache_ckv.shape[1]
    scale = float((nope + rope_dim) ** -0.5)

    pos = jnp.concatenate([jnp.arange(n_prompt, dtype=jnp.int32),
                           past + jnp.arange(t - n_prompt, dtype=jnp.int32) % dec_seq])
    cos_t, sin_t = _rope_tables(pos, rope_dim)

    hw = -(-(q_lora + kv_lora + LANES) // (2 * LANES)) * (2 * LANES)
    w_in_p = jnp.pad(w_in, ((0, 0), (0, hw - w_in.shape[1])))
    h = _matmul(xb, w_in_p, out_dtype=F32, tn=256, name="mla_in")
    cq, ckv, ckv_b, kr, krp_b = _mla_latent(h, q_g, kv_g, cos_t, sin_t, q_lora, kv_lora, rope_dim)

    w_q = jnp.pad(w_uq, ((0, 0), (0, 0), (0, 2 * LANES - nope - rope_dim))).reshape(q_lora, n_heads * 2 * LANES)
    q = _q_proj(cq, w_q, cos_t, sin_t)

    kn = _matmul(ckv_b[:n_prompt], w_uk.reshape(kv_lora, n_heads * nope), out_dtype=BF16, name="k_nope")
    v = _matmul(ckv_b[:n_prompt], w_uv.reshape(kv_lora, n_heads * v_dim), out_dtype=BF16, name="v_proj")
    attn_p = _prompt_attn(q, kn, krp_b, v, n_prompt, n_heads, scale)

    w_uk_h = jnp.transpose(w_uk, (1, 0, 2))
    w_uv_h = jnp.transpose(w_uv, (1, 0, 2))
    q_lat = _q_latent(q, w_uk_h, n_prompt)
    cache_krp = jnp.pad(cache_krope, ((0, 0), (0, 0), (0, LANES - rope_dim)))
    o_lat = _sample_attn(q_lat, cache_ckv, cache_krp, ckv_b, krp_b, n_prompt, scale, dec_seq)
    attn_s = _o_latent(o_lat, w_uv_h)
    return jnp.concatenate([attn_p, attn_s], axis=0), ckv, kr


def kernel(x_prompt, x_sample, cache_mla_ckv, cache_mla_krope, ln_mix_g, ln_mix_b, ln_ffn_g, ln_ffn_b,
           a_w_in, a_ln_v_g, a_ln_v_b, a_w_s, a_b_s, a_w_out,
           b_w_in, b_q_norm_g, b_kv_norm_g, b_w_uq, b_w_uk, b_w_uv, b_w_out,
           router_w, router_b, exp_w_gu, exp_b_gu, exp_w_down, exp_b_down):
    depth = ln_mix_g.shape[0]
    n_mixers = 2
    alpha = float((2 * depth) ** 0.25)
    bsz, seq, d = x_prompt.shape
    dec_b, dec_seq, _ = x_sample.shape
    n_prompt = bsz * seq
    n_sample = dec_b * dec_seq
    assert bsz == 1 and n_prompt % GMLP_CHUNK == 0 and GMLP_CHUNK % dec_seq == 0

    x = jnp.concatenate([x_prompt.reshape(n_prompt, d), x_sample.reshape(n_sample, d)], axis=0)
    xb = x.astype(BF16)
    v_s, ckv_p, kr_p, ckv_s, kr_s = [], [], [], [], []
    for i in range(depth):
        j = i // n_mixers
        if i % n_mixers == 0:
            u, v = _gmlp_in(xb, a_w_in[j])
            gated, vn = _gmlp_gate(u, v, a_ln_v_g[j], a_ln_v_b[j], a_w_s[j], a_b_s[j], n_prompt, dec_seq)
            pre = _matmul(gated, a_w_out[j], out_dtype=F32, res=x, alpha=alpha, name="gmlp_out")
            v_s.append(vn[n_prompt:].reshape(dec_b, dec_seq, -1))
        else:
            attn, ckv, kr = _mla_mixer(xb, n_prompt, dec_seq, cache_mla_ckv[j], cache_mla_krope[j],
                                       b_w_in[j], b_q_norm_g[j], b_kv_norm_g[j], b_w_uq[j], b_w_uk[j],
                                       b_w_uv[j])
            pre = _matmul(attn, b_w_out[j], out_dtype=F32, res=x, alpha=alpha, name="mla_out")
            ckv_p.append(ckv[:n_prompt].reshape(bsz, seq, -1))
            kr_p.append(kr[:n_prompt].reshape(bsz, seq, -1))
            ckv_s.append(ckv[n_prompt:].reshape(dec_b, dec_seq, -1))
            kr_s.append(kr[n_prompt:].reshape(dec_b, dec_seq, -1))
        x, xb = _moe_layer(pre, ln_mix_g[i], ln_mix_b[i], router_w[i], router_b[i], exp_w_gu[i],
                           exp_b_gu[i], exp_w_down[i], exp_b_down[i], ln_ffn_g[i], ln_ffn_b[i], alpha)

    return (x[:n_prompt].reshape(bsz, seq, d), x[n_prompt:].reshape(dec_b, dec_seq, d),
            jnp.stack(v_s), jnp.stack(ckv_p), jnp.stack(kr_p), jnp.stack(ckv_s), jnp.stack(kr_s))
```

```python
import functools

import numpy as np
import jax
import jax.numpy as jnp
from jax import lax
from jax.experimental import pallas as pl
from jax.experimental.pallas import tpu as pltpu

F32 = jnp.float32
BF16 = jnp.bfloat16

CHUNK = 64
GMLP_CHUNK = 128
GMLP_GROUPS = 8
TOP_K = 4
ROPE_THETA = 10000.0
NEG_INF = -1e30
SWIGLU_LIMIT = 7.0
SWIGLU_ALPHA = 1.702
LN_EPS = 1e-5
RMS_EPS = 1e-6

LANES = 128
MOE_SUPER_ROWS = 1024
MOE_SUB_ROWS = 256
DMA_UNROLL = 8
VMEM_LIMIT = 56 * 1024 * 1024


def _tile(dim, pref):
    if dim <= pref:
        return dim
    t = pref
    while dim % t:
        t //= 2
    return t


def _params(*sem):
    return pltpu.CompilerParams(dimension_semantics=sem, vmem_limit_bytes=VMEM_LIMIT)


def _layer_norm(x, g, b):
    mu = jnp.mean(x, axis=-1, keepdims=True)
    xc = x - mu
    var = jnp.mean(xc * xc, axis=-1, keepdims=True)
    return xc * lax.rsqrt(var + LN_EPS) * g + b


def _rms_norm(x, g):
    return x * lax.rsqrt(jnp.mean(x * x, axis=-1, keepdims=True) + RMS_EPS) * g


def _gelu(x):
    return 0.5 * x * (1.0 + lax.erf(x * np.float32(1.0 / np.sqrt(2.0))))


def _swap_rope_halves(x):
    half = 32
    lane = lax.broadcasted_iota(jnp.int32, x.shape, 1)
    fwd = pltpu.roll(x, LANES - half, 1)
    bwd = pltpu.roll(x, half, 1)
    return jnp.where((lane % (2 * half)) < half, fwd, bwd)


def _mm_kernel(*refs, alpha, has_res):
    if has_res:
        x_ref, w_ref, r_ref, o_ref = refs
    else:
        x_ref, w_ref, o_ref = refs
    acc = jnp.dot(x_ref[...], w_ref[...].astype(BF16), preferred_element_type=F32)
    if has_res:
        acc = alpha * r_ref[...] + acc
    o_ref[...] = acc.astype(o_ref.dtype)


def _matmul(x, w, *, out_dtype, res=None, alpha=1.0, tm=1024, tn=512, name):
    m, k = x.shape
    n = w.shape[1]
    tm, tn = _tile(m, tm), _tile(n, tn)
    in_specs = [pl.BlockSpec((tm, k), lambda i, j: (i, 0)),
                pl.BlockSpec((k, tn), lambda i, j: (0, j))]
    args = [x, w]
    if res is not None:
        in_specs.append(pl.BlockSpec((tm, tn), lambda i, j: (i, j)))
        args.append(res)
    return pl.pallas_call(
        functools.partial(_mm_kernel, alpha=alpha, has_res=res is not None),
        grid=(m // tm, n // tn),
        in_specs=in_specs,
        out_specs=pl.BlockSpec((tm, tn), lambda i, j: (i, j)),
        out_shape=jax.ShapeDtypeStruct((m, n), out_dtype),
        compiler_params=_params("parallel", "arbitrary"),
        name=name,
    )(*args)


def _gmlp_in_kernel(x_ref, wu_ref, wv_ref, u_ref, v_ref):
    x = x_ref[...]
    u_ref[...] = _gelu(jnp.dot(x, wu_ref[...].astype(BF16), preferred_element_type=F32))
    v_ref[...] = _gelu(jnp.dot(x, wv_ref[...].astype(BF16), preferred_element_type=F32))


def _gmlp_in(xb, w_in):
    m, k = xb.shape
    width = w_in.shape[1] // 2
    tm, tn = _tile(m, 1024), _tile(width, 256)
    nj = width // tn
    return pl.pallas_call(
        _gmlp_in_kernel,
        grid=(m // tm, nj),
        in_specs=[pl.BlockSpec((tm, k), lambda i, j: (i, 0)),
                  pl.BlockSpec((k, tn), lambda i, j: (0, j)),
                  pl.BlockSpec((k, tn), lambda i, j: (0, j + nj))],
        out_specs=[pl.BlockSpec((tm, tn), lambda i, j: (i, j)),
                   pl.BlockSpec((tm, tn), lambda i, j: (i, j))],
        out_shape=[jax.ShapeDtypeStruct((m, width), F32)] * 2,
        compiler_params=_params("parallel", "arbitrary"),
        name="gmlp_in",
    )(xb, w_in, w_in)


def _gmlp_gate_kernel(u_ref, v_ref, g_ref, b_ref, ws_ref, bs_ref, o_ref, vn_ref, *, n_prompt_tiles,
                      sample_shift):
    i = pl.program_id(0)
    rows = u_ref.shape[0]
    gdim = u_ref.shape[1] // GMLP_GROUPS
    vn = _layer_norm(v_ref[...], g_ref[...], b_ref[...])
    vn_ref[...] = vn
    vnb = vn.astype(BF16)
    shift = jnp.where(i >= n_prompt_tiles, sample_shift, int(np.log2(GMLP_CHUNK)))
    r = lax.broadcasted_iota(jnp.int32, (rows, rows), 0)
    c = lax.broadcasted_iota(jnp.int32, (rows, rows), 1)
    keep = (c <= r) & ((r >> shift) == (c >> shift))
    bias = bs_ref[...]
    for g in range(GMLP_GROUPS):
        w = jnp.where(keep, ws_ref[g], 0.0).astype(BF16)
        s = jnp.dot(w, vnb[:, g * gdim:(g + 1) * gdim], preferred_element_type=F32)
        s = s + bias[:, g:g + 1]
        o_ref[:, g * gdim:(g + 1) * gdim] = (u_ref[:, g * gdim:(g + 1) * gdim] * s).astype(o_ref.dtype)


def _gmlp_gate(u, v, ln_g, ln_b, w_s, b_s, n_prompt, dec_seq):
    t, width = u.shape
    rows = GMLP_CHUNK
    reps = rows // dec_seq
    w_prompt = w_s
    w_sample = jnp.tile(w_s[:, :dec_seq, :dec_seq], (1, reps, reps))
    b_prompt = b_s.T
    b_sample = jnp.tile(b_s[:, :dec_seq].T, (reps, 1))
    w_all = jnp.stack([w_prompt, w_sample])
    b_all = jnp.stack([b_prompt, b_sample])
    n_prompt_tiles = n_prompt // rows
    sel = lambda i: (i >= n_prompt_tiles).astype(jnp.int32)
    return pl.pallas_call(
        functools.partial(_gmlp_gate_kernel, n_prompt_tiles=n_prompt_tiles,
                          sample_shift=int(np.log2(dec_seq))),
        grid=(t // rows,),
        in_specs=[pl.BlockSpec((rows, width), lambda i: (i, 0)),
                  pl.BlockSpec((rows, width), lambda i: (i, 0)),
                  pl.BlockSpec((1, width), lambda i: (0, 0)),
                  pl.BlockSpec((1, width), lambda i: (0, 0)),
                  pl.BlockSpec((None, GMLP_GROUPS, rows, rows), lambda i: (sel(i), 0, 0, 0)),
                  pl.BlockSpec((None, rows, GMLP_GROUPS), lambda i: (sel(i), 0, 0))],
        out_specs=[pl.BlockSpec((rows, width), lambda i: (i, 0)),
                   pl.BlockSpec((rows, width), lambda i: (i, 0))],
        out_shape=[jax.ShapeDtypeStruct((t, width), BF16), jax.ShapeDtypeStruct((t, width), F32)],
        compiler_params=_params("parallel"),
        name="gmlp_gate",
    )(u, v, ln_g.reshape(1, -1), ln_b.reshape(1, -1), w_all, b_all)


def _ln_router_kernel(p_ref, g_ref, b_ref, rw_ref, rb_ref, y_ref, idx_ref, gate_ref):
    y = _layer_norm(p_ref[...], g_ref[...], b_ref[...])
    y_ref[...] = y
    logits = jnp.dot(y, rw_ref[...], precision=lax.Precision.HIGHEST,
                     preferred_element_type=F32) + rb_ref[...]
    n_e = logits.shape[1]
    col = lax.broadcasted_iota(jnp.int32, logits.shape, 1)
    vals, idxs = [], []
    cur = logits
    for _ in range(TOP_K):
        mx = jnp.max(cur, axis=-1, keepdims=True)
        am = jnp.min(jnp.where(cur == mx, col, n_e), axis=-1, keepdims=True)
        vals.append(mx)
        idxs.append(am)
        cur = jnp.where(col == am, -jnp.inf, cur)
    es = [jnp.exp(v - vals[0]) for v in vals]
    denom = es[0]
    for e in es[1:]:
        denom = denom + e
    kcol = lax.broadcasted_iota(jnp.int32, (logits.shape[0], TOP_K), 1)
    idx_out = jnp.zeros((logits.shape[0], TOP_K), jnp.int32)
    gate_out = jnp.zeros((logits.shape[0], TOP_K), F32)
    for k in range(TOP_K):
        idx_out = jnp.where(kcol == k, idxs[k], idx_out)
        gate_out = jnp.where(kcol == k, es[k] / denom, gate_out)
    idx_ref[...] = idx_out
    gate_ref[...] = gate_out


def _ln_router(pre, g, b, rw, rb):
    t, d = pre.shape
    n_e = rw.shape[1]
    tm = _tile(t, 256)
    return pl.pallas_call(
        _ln_router_kernel,
        grid=(t // tm,),
        in_specs=[pl.BlockSpec((tm, d), lambda i: (i, 0)),
                  pl.BlockSpec((1, d), lambda i: (0, 0)),
                  pl.BlockSpec((1, d), lambda i: (0, 0)),
                  pl.BlockSpec((d, n_e), lambda i: (0, 0)),
                  pl.BlockSpec((1, n_e), lambda i: (0, 0))],
        out_specs=[pl.BlockSpec((tm, d), lambda i: (i, 0)),
                   pl.BlockSpec((tm, TOP_K), lambda i: (i, 0)),
                   pl.BlockSpec((tm, TOP_K), lambda i: (i, 0))],
        out_shape=[jax.ShapeDtypeStruct((t, d), F32),
                   jax.ShapeDtypeStruct((t, TOP_K), jnp.int32),
                   jax.ShapeDtypeStruct((t, TOP_K), F32)],
        compiler_params=_params("parallel"),
        name="ln_router",
    )(pre, g.reshape(1, -1), b.reshape(1, -1), rw, rb.reshape(1, -1))


def _route_tables(top_idx, n_e, n_sb):
    t = top_idx.shape[0]
    flat_e = top_idx.reshape(-1)
    onehot = (flat_e[:, None] == jnp.arange(n_e, dtype=jnp.int32)[None, :]).astype(jnp.int32)
    csum = jnp.cumsum(onehot, axis=0)
    rank = jnp.take_along_axis(csum, flat_e[:, None], axis=1)[:, 0] - 1
    counts = csum[-1]
    sb_per_e = (counts + MOE_SUPER_ROWS - 1) // MOE_SUPER_ROWS
    sb_end = jnp.cumsum(sb_per_e)
    sb_first = sb_end - sb_per_e
    n_used = sb_end[-1]
    sb = jnp.arange(n_sb, dtype=jnp.int32)
    sb_c = jnp.minimum(sb, n_used - 1)
    e_of = jnp.minimum(jnp.searchsorted(sb_end, sb_c, side="right"), n_e - 1).astype(jnp.int32)
    rows = jnp.clip(counts[e_of] - (sb_c - sb_first[e_of]) * MOE_SUPER_ROWS, 0, MOE_SUPER_ROWS)
    rows = jnp.where(sb < n_used, rows, 0).astype(jnp.int32)
    dest = (sb_first[flat_e] * MOE_SUPER_ROWS + rank).astype(jnp.int32)
    tok = (jnp.arange(t * TOP_K, dtype=jnp.int32) // TOP_K)
    slot_tok = jnp.zeros((n_sb * MOE_SUPER_ROWS,), jnp.int32).at[dest].set(tok)
    return e_of, rows, slot_tok, dest


def _row_blocks(rows, compute, zero):
    sub = MOE_SUB_ROWS
    nb = (rows + sub - 1) // sub
    n_big = nb // 2

    def big(s, c):
        compute(pl.multiple_of(s * 2 * sub, 2 * sub), 2 * sub)
        return c
    lax.fori_loop(0, n_big, big, 0)

    @pl.when(nb % 2 == 1)
    def _():
        compute(pl.multiple_of(n_big * 2 * sub, sub), sub)

    def zero_body(s, c):
        zero(pl.multiple_of(s * sub, sub))
        return c
    lax.fori_loop(nb, MOE_SUPER_ROWS // sub, zero_body, 0)


def _moe_up_kernel(sb_e, sb_rows, slot_tok, y_hbm, wg_ref, wu_ref, bg_ref, bu_ref, act_ref,
                   xf32, xb16, wb16, sem):
    sb = pl.program_id(0)
    j = pl.program_id(1)
    n_sb = pl.num_programs(0)
    tn = wg_ref.shape[1]
    rows = sb_rows[sb]

    def row_copy(tok, r):
        return pltpu.make_async_copy(y_hbm.at[pl.ds(tok, 1)], xf32.at[pl.ds(r, 1)], sem.at[0])

    def n_groups(sb_i):
        return (sb_rows[sb_i] + DMA_UNROLL - 1) // DMA_UNROLL

    def issue(sb_i):
        base = sb_i * MOE_SUPER_ROWS

        def body_grp(g, c):
            r0 = g * DMA_UNROLL
            for u in range(DMA_UNROLL):
                row_copy(slot_tok[base + r0 + u], r0 + u).start()
            return c
        lax.fori_loop(0, n_groups(sb_i), body_grp, 0)

    @pl.when((sb == 0) & (j == 0))
    def _():
        xf32[...] = jnp.zeros_like(xf32)
        issue(0)

    @pl.when(j == 0)
    def _():
        @pl.when(rows > 0)
        def _():
            n = pl.multiple_of(n_groups(sb) * DMA_UNROLL, DMA_UNROLL)
            pltpu.make_async_copy(y_hbm.at[pl.ds(0, n)], xf32.at[pl.ds(0, n)], sem.at[0]).wait()
        xb16[...] = xf32[...].astype(BF16)

        @pl.when(sb + 1 < n_sb)
        def _():
            issue(sb + 1)

    @pl.when(rows > 0)
    def _():
        wb16[:, :tn] = wg_ref[...].astype(BF16)
        wb16[:, tn:] = wu_ref[...].astype(BF16)

    def compute(r0, n):
        h = jnp.dot(xb16[pl.ds(r0, n), :], wb16[...], preferred_element_type=F32)
        gate = jnp.minimum(h[:, :tn] + bg_ref[...], SWIGLU_LIMIT)
        up = jnp.clip(h[:, tn:] + bu_ref[...], -SWIGLU_LIMIT, SWIGLU_LIMIT)
        a = (up + 1.0) * gate * jax.nn.sigmoid(SWIGLU_ALPHA * gate)
        act_ref[pl.ds(r0, n), :] = a.astype(act_ref.dtype)

    def zero(r0):
        act_ref[pl.ds(r0, MOE_SUB_ROWS), :] = jnp.zeros((MOE_SUB_ROWS, tn), act_ref.dtype)

    _row_blocks(rows, compute, zero)


def _moe_up(y, w_gu, b_gu, layer, sb_e, sb_rows, slot_tok):
    t, d = y.shape
    _, n_e, _, ff2 = w_gu.shape
    ff = ff2 // 2
    n_sb = sb_e.shape[0]
    tn = _tile(ff, 256)
    nj = ff // tn
    b4 = b_gu.reshape(b_gu.shape[0], n_e, 1, ff2)

    def jw(sb, j, rows):
        return jnp.where(rows[sb] > 0, j, nj - 1)

    grid_spec = pltpu.PrefetchScalarGridSpec(
        num_scalar_prefetch=3,
        grid=(n_sb, nj),
        in_specs=[pl.BlockSpec(memory_space=pl.ANY),
                  pl.BlockSpec((None, None, d, tn), lambda sb, j, e, rows, st: (layer, e[sb], 0, jw(sb, j, rows))),
                  pl.BlockSpec((None, None, d, tn), lambda sb, j, e, rows, st: (layer, e[sb], 0, jw(sb, j, rows) + nj)),
                  pl.BlockSpec((None, None, 1, tn), lambda sb, j, e, rows, st: (layer, e[sb], 0, jw(sb, j, rows))),
                  pl.BlockSpec((None, None, 1, tn), lambda sb, j, e, rows, st: (layer, e[sb], 0, jw(sb, j, rows) + nj))],
        out_specs=pl.BlockSpec((MOE_SUPER_ROWS, tn), lambda sb, j, e, rows, st: (sb, j)),
        scratch_shapes=[pltpu.VMEM((MOE_SUPER_ROWS, d), F32),
                        pltpu.VMEM((MOE_SUPER_ROWS, d), BF16),
                        pltpu.VMEM((d, 2 * tn), BF16),
                        pltpu.SemaphoreType.DMA((1,))],
    )
    return pl.pallas_call(
        _moe_up_kernel,
        grid_spec=grid_spec,
        out_shape=jax.ShapeDtypeStruct((n_sb * MOE_SUPER_ROWS, ff), BF16),
        compiler_params=_params("arbitrary", "arbitrary"),
        name="moe_up",
    )(sb_e, sb_rows, slot_tok, y, w_gu, w_gu, b4, b4)


def _moe_down_kernel(sb_e, sb_rows, a_ref, w_ref, b_ref, o_ref, wb16):
    sb = pl.program_id(0)
    tn = w_ref.shape[1]
    rows = sb_rows[sb]

    @pl.when(rows > 0)
    def _():
        wb16[...] = w_ref[...].astype(BF16)

    def compute(r0, n):
        o_ref[pl.ds(r0, n), :] = jnp.dot(a_ref[pl.ds(r0, n), :], wb16[...],
                                         preferred_element_type=F32) + b_ref[...]

    def zero(r0):
        o_ref[pl.ds(r0, MOE_SUB_ROWS), :] = jnp.zeros((MOE_SUB_ROWS, tn), F32)

    _row_blocks(rows, compute, zero)


def _moe_down(act, w_down, b_down, layer, sb_e, sb_rows):
    n_slots, ff = act.shape
    _, n_e, _, d = w_down.shape
    n_sb = sb_e.shape[0]
    tn = _tile(d, 1024)
    nj = d // tn
    b4 = b_down.reshape(b_down.shape[0], n_e, 1, d)

    def jw(sb, j, rows):
        return jnp.where(rows[sb] > 0, j, nj - 1)

    grid_spec = pltpu.PrefetchScalarGridSpec(
        num_scalar_prefetch=2,
        grid=(n_sb, nj),
        in_specs=[pl.BlockSpec((MOE_SUPER_ROWS, ff), lambda sb, j, e, rows: (sb, 0)),
                  pl.BlockSpec((None, None, ff, tn), lambda sb, j, e, rows: (layer, e[sb], 0, jw(sb, j, rows))),
                  pl.BlockSpec((None, None, 1, tn), lambda sb, j, e, rows: (layer, e[sb], 0, jw(sb, j, rows)))],
        out_specs=pl.BlockSpec((MOE_SUPER_ROWS, tn), lambda sb, j, e, rows: (sb, j)),
        scratch_shapes=[pltpu.VMEM((ff, tn), BF16)],
    )
    return pl.pallas_call(
        _moe_down_kernel,
        grid_spec=grid_spec,
        out_shape=jax.ShapeDtypeStruct((n_slots, d), F32),
        compiler_params=_params("arbitrary", "arbitrary"),
        name="moe_down",
    )(sb_e, sb_rows, act, w_down, b4)


def _moe_combine_kernel(dest, ys_hbm, gate_ref, y_ref, g_ref, b_ref, o_ref, ob_ref, gbuf, sem, *, alpha):
    i = pl.program_id(0)
    n = pl.num_programs(0)
    tm = y_ref.shape[0]
    slot = i % 2

    def issue(tile, s):
        base = tile * tm * TOP_K

        def body(r, c):
            for k in range(TOP_K):
                pltpu.make_async_copy(ys_hbm.at[pl.ds(dest[base + r * TOP_K + k], 1)],
                                      gbuf.at[s, k, pl.ds(r, 1)], sem.at[s]).start()
            return c
        lax.fori_loop(0, tm, body, 0, unroll=DMA_UNROLL // TOP_K)

    @pl.when(i == 0)
    def _():
        issue(0, 0)

    @pl.when(i + 1 < n)
    def _():
        issue(i + 1, 1 - slot)

    for k in range(TOP_K):
        pltpu.make_async_copy(ys_hbm.at[pl.ds(0, tm)], gbuf.at[slot, k], sem.at[slot]).wait()

    gates = gate_ref[...]
    moe = gates[:, 0:1] * gbuf[slot, 0]
    for k in range(1, TOP_K):
        moe = moe + gates[:, k:k + 1] * gbuf[slot, k]
    out = _layer_norm(alpha * y_ref[...] + moe, g_ref[...], b_ref[...])
    o_ref[...] = out
    ob_ref[...] = out.astype(ob_ref.dtype)


def _moe_combine(ys, dest, gates, y, g, b, alpha):
    t, d = y.shape
    tm = _tile(t, 128)
    grid_spec = pltpu.PrefetchScalarGridSpec(
        num_scalar_prefetch=1,
        grid=(t // tm,),
        in_specs=[pl.BlockSpec(memory_space=pl.ANY),
                  pl.BlockSpec((tm, TOP_K), lambda i, dst: (i, 0)),
                  pl.BlockSpec((tm, d), lambda i, dst: (i, 0)),
                  pl.BlockSpec((1, d), lambda i, dst: (0, 0)),
                  pl.BlockSpec((1, d), lambda i, dst: (0, 0))],
        out_specs=[pl.BlockSpec((tm, d), lambda i, dst: (i, 0)),
                   pl.BlockSpec((tm, d), lambda i, dst: (i, 0))],
        scratch_shapes=[pltpu.VMEM((2, TOP_K, tm, d), F32),
                        pltpu.SemaphoreType.DMA((2,))],
    )
    return pl.pallas_call(
        functools.partial(_moe_combine_kernel, alpha=alpha),
        grid_spec=grid_spec,
        out_shape=[jax.ShapeDtypeStruct((t, d), F32), jax.ShapeDtypeStruct((t, d), BF16)],
        compiler_params=_params("arbitrary"),
        name="moe_combine",
    )(dest, ys, gates, y, g.reshape(1, -1), b.reshape(1, -1))


def _moe_layer(pre, ln_g, ln_b, rw, rb, w_gu, b_gu, w_down, b_down, layer, ffn_g, ffn_b, alpha):
    t = pre.shape[0]
    n_e = rw.shape[1]
    y, top_idx, gates = _ln_router(pre, ln_g, ln_b, rw, rb)
    n_sb = (t * TOP_K) // MOE_SUPER_ROWS + n_e
    sb_e, sb_rows, slot_tok, dest = _route_tables(top_idx, n_e, n_sb)
    act = _moe_up(y, w_gu, b_gu, layer, sb_e, sb_rows, slot_tok)
    ys = _moe_down(act, w_down, b_down, layer, sb_e, sb_rows)
    return _moe_combine(ys, dest, gates, y, ffn_g, ffn_b, alpha)


def _mla_latent_kernel(h_ref, qg_ref, kvg_ref, cos_ref, sin_ref, cq_ref, ckv_ref, ckvb_ref, kr_ref,
                       krp_ref, *, q_lora, kv_lora, rope_dim):
    h = h_ref[...]
    cq_ref[...] = _rms_norm(h[:, :q_lora], qg_ref[...]).astype(cq_ref.dtype)
    ckv = _rms_norm(h[:, q_lora:q_lora + kv_lora], kvg_ref[...])
    ckv_ref[...] = ckv
    ckvb_ref[...] = ckv.astype(ckvb_ref.dtype)
    x = h[:, q_lora + kv_lora:q_lora + kv_lora + LANES]
    kr = x * cos_ref[...] + _swap_rope_halves(x) * sin_ref[...]
    kr_ref[...] = kr[:, :rope_dim]
    krp_ref[...] = kr.astype(krp_ref.dtype)


def _mla_latent(h, q_g, kv_g, cos_t, sin_t, q_lora, kv_lora, rope_dim):
    t, hw = h.shape
    tm = _tile(t, 512)
    return pl.pallas_call(
        functools.partial(_mla_latent_kernel, q_lora=q_lora, kv_lora=kv_lora, rope_dim=rope_dim),
        grid=(t // tm,),
        in_specs=[pl.BlockSpec((tm, hw), lambda i: (i, 0)),
                  pl.BlockSpec((1, q_lora), lambda i: (0, 0)),
                  pl.BlockSpec((1, kv_lora), lambda i: (0, 0)),
                  pl.BlockSpec((tm, LANES), lambda i: (i, 0)),
                  pl.BlockSpec((tm, LANES), lambda i: (i, 0))],
        out_specs=[pl.BlockSpec((tm, q_lora), lambda i: (i, 0)),
                   pl.BlockSpec((tm, kv_lora), lambda i: (i, 0)),
                   pl.BlockSpec((tm, kv_lora), lambda i: (i, 0)),
                   pl.BlockSpec((tm, rope_dim), lambda i: (i, 0)),
                   pl.BlockSpec((tm, LANES), lambda i: (i, 0))],
        out_shape=[jax.ShapeDtypeStruct((t, q_lora), BF16),
                   jax.ShapeDtypeStruct((t, kv_lora), F32),
                   jax.ShapeDtypeStruct((t, kv_lora), BF16),
                   jax.ShapeDtypeStruct((t, rope_dim), F32),
                   jax.ShapeDtypeStruct((t, LANES), BF16)],
        compiler_params=_params("parallel"),
        name="mla_latent",
    )(h, q_g.reshape(1, -1), kv_g.reshape(1, -1), cos_t, sin_t)


def _q_proj_kernel(x_ref, w_ref, cos_ref, sin_ref, o_ref, *, scale):
    acc = jnp.dot(x_ref[...], w_ref[...].astype(BF16), preferred_element_type=F32) * scale
    cos, sin = cos_ref[...], sin_ref[...]
    for g in range(acc.shape[1] // LANES):
        blk = acc[:, g * LANES:(g + 1) * LANES]
        if g % 2:
            blk = blk * cos + _swap_rope_halves(blk) * sin
        o_ref[:, g * LANES:(g + 1) * LANES] = blk.astype(o_ref.dtype)


def _q_proj(cq, w_q, cos_t, sin_t, scale):
    t, k = cq.shape
    n = w_q.shape[1]
    tm, tn = _tile(t, 1024), _tile(n, 512)
    return pl.pallas_call(
        functools.partial(_q_proj_kernel, scale=scale),
        grid=(t // tm, n // tn),
        in_specs=[pl.BlockSpec((tm, k), lambda i, j: (i, 0)),
                  pl.BlockSpec((k, tn), lambda i, j: (0, j)),
                  pl.BlockSpec((tm, LANES), lambda i, j: (i, 0)),
                  pl.BlockSpec((tm, LANES), lambda i, j: (i, 0))],
        out_specs=pl.BlockSpec((tm, tn), lambda i, j: (i, j)),
        out_shape=jax.ShapeDtypeStruct((t, n), BF16),
        compiler_params=_params("parallel", "arbitrary"),
        name="q_proj",
    )(cq, w_q, cos_t, sin_t)


def _prompt_attn_kernel(q_ref, kn_ref, kr_ref, v_ref, o_ref, kcat, vcat, s_e0, s_e1, s_o0, s_o1,
                        m_0, m_1, acc_0, acc_1, *, tq, tk):
    qi = pl.program_id(1)
    hd = 2 * LANES
    s_even, s_odd = (s_e0, s_e1), (s_o0, s_o1)
    m_refs, acc_refs = (m_0, m_1), (acc_0, acc_1)

    @pl.when(qi == 0)
    def _():
        for a in range(2):
            kcat[a, :, :LANES] = kn_ref[:, a * LANES:(a + 1) * LANES]
            kcat[a, :, LANES:] = kr_ref[...]
            vcat[a, :, :LANES] = v_ref[:, a * LANES:(a + 1) * LANES]
            vcat[a, :, LANES:] = jnp.ones((vcat.shape[1], LANES), vcat.dtype)

    for a in range(2):
        m_refs[a][...] = jnp.full(m_refs[a].shape, NEG_INF, F32)
        acc_refs[a][...] = jnp.zeros(acc_refs[a].shape, F32)

    def scores(kb, bufs):
        k0 = pl.multiple_of(kb * tk, tk)
        for a in range(2):
            bufs[a][...] = lax.dot_general(q_ref[:, a * hd:(a + 1) * hd], kcat[a, pl.ds(k0, tk), :],
                                           (((1,), (1,)), ((), ())), preferred_element_type=F32)

    def softmax_pv(kb, bufs, masked):
        k0 = pl.multiple_of(kb * tk, tk)
        if masked:
            r = qi * tq + lax.broadcasted_iota(jnp.int32, (tq, tk), 0)
            c = k0 + lax.broadcasted_iota(jnp.int32, (tq, tk), 1)
            ok = (c // CHUNK) <= (r // CHUNK)
        for a in range(2):
            s = bufs[a][...]
            if masked:
                s = jnp.where(ok, s, NEG_INF)
            m_prev = m_refs[a][...]
            m_new = jnp.maximum(m_prev, jnp.max(s, axis=-1, keepdims=True))
            p = jnp.exp(s - m_new).astype(BF16)
            acc_refs[a][...] = jnp.exp(m_prev - m_new) * acc_refs[a][...] + jnp.dot(
                p, vcat[a, pl.ds(k0, tk), :], preferred_element_type=F32)
            m_refs[a][...] = m_new

    n_pair = (qi * tq) // (2 * tk)
    scores(0, s_even)

    def body(i, carry):
        kb = 2 * i
        scores(kb + 1, s_odd)
        softmax_pv(kb, s_even, False)
        scores(kb + 2, s_even)
        softmax_pv(kb + 1, s_odd, False)
        return carry
    lax.fori_loop(0, n_pair, body, 0)
    kb = 2 * n_pair
    scores(kb + 1, s_odd)
    softmax_pv(kb, s_even, True)
    softmax_pv(kb + 1, s_odd, True)
    for a in range(2):
        o_ref[:, a * LANES:(a + 1) * LANES] = (
            acc_refs[a][:, :LANES] / acc_refs[a][:, LANES:]).astype(o_ref.dtype)


def _prompt_attn(q, kn, krp, v, n_prompt, n_heads):
    tq = _tile(n_prompt, 1024)
    tk = tq // 2
    assert tk % CHUNK == 0 and n_prompt % tq == 0
    return pl.pallas_call(
        functools.partial(_prompt_attn_kernel, tq=tq, tk=tk),
        grid=(n_heads // 2, n_prompt // tq),
        in_specs=[pl.BlockSpec((tq, 4 * LANES), lambda p, i: (i, p)),
                  pl.BlockSpec((n_prompt, 2 * LANES), lambda p, i: (0, p), pipeline_mode=pl.Buffered(1)),
                  pl.BlockSpec((n_prompt, LANES), lambda p, i: (0, 0), pipeline_mode=pl.Buffered(1)),
                  pl.BlockSpec((n_prompt, 2 * LANES), lambda p, i: (0, p), pipeline_mode=pl.Buffered(1))],
        out_specs=pl.BlockSpec((tq, 2 * LANES), lambda p, i: (i, p)),
        out_shape=jax.ShapeDtypeStruct((n_prompt, n_heads * LANES), BF16),
        scratch_shapes=([pltpu.VMEM((2, n_prompt, 2 * LANES), BF16)] * 2
                        + [pltpu.VMEM((tq, tk), F32)] * 4
                        + [pltpu.VMEM((tq, 1), F32)] * 2
                        + [pltpu.VMEM((tq, 2 * LANES), F32)] * 2),
        compiler_params=_params("arbitrary", "arbitrary"),
        name="prompt_attn",
    )(q, kn, krp, v)


def _q_latent_kernel(q_ref, wuk_ref, o_ref, *, kv_lora):
    qn = q_ref[:, :LANES]
    q_lat = lax.dot_general(qn, wuk_ref[...].astype(BF16), (((1,), (1,)), ((), ())),
                            preferred_element_type=F32)
    o_ref[:, :kv_lora] = q_lat.astype(o_ref.dtype)
    o_ref[:, kv_lora:] = q_ref[:, LANES:]


def _q_latent(q, w_uk_h, n_prompt):
    n_heads, kv_lora, _ = w_uk_h.shape
    n_s = q.shape[0] - n_prompt
    assert n_prompt % n_s == 0
    rb = n_prompt // n_s
    return pl.pallas_call(
        functools.partial(_q_latent_kernel, kv_lora=kv_lora),
        grid=(n_heads,),
        in_specs=[pl.BlockSpec((n_s, 2 * LANES), lambda h: (rb, h)),
                  pl.BlockSpec((None, kv_lora, LANES), lambda h: (h, 0, 0))],
        out_specs=pl.BlockSpec((None, n_s, kv_lora + LANES), lambda h: (h, 0, 0)),
        out_shape=jax.ShapeDtypeStruct((n_heads, n_s, kv_lora + LANES), BF16),
        compiler_params=_params("parallel"),
        name="q_latent",
    )(q, w_uk_h)


def _sample_attn_kernel(q_ref, cc_ref, ck_ref, nc_ref, nk_ref, o_ref, kcat, *, past, dec_seq, kv_lora,
                        n_keys_pad):
    n_heads = q_ref.shape[0]
    n_q = n_heads * dec_seq
    kw = kv_lora + LANES
    kcat[0:past, 0:kv_lora] = cc_ref[...].astype(BF16)
    kcat[0:past, kv_lora:kw] = ck_ref[...].astype(BF16)
    kcat[past:past + dec_seq, 0:kv_lora] = nc_ref[...]
    kcat[past:past + dec_seq, kv_lora:kw] = nk_ref[...]
    kcat[past + dec_seq:n_keys_pad, :] = jnp.zeros((n_keys_pad - past - dec_seq, kw), BF16)
    q = q_ref[...].reshape(n_q, kw)
    s = lax.dot_general(q, kcat[...], (((1,), (1,)), ((), ())), preferred_element_type=F32)
    q_pos = past + lax.broadcasted_iota(jnp.int32, s.shape, 0) % dec_seq
    k_pos = lax.broadcasted_iota(jnp.int32, s.shape, 1)
    ok = ((k_pos // CHUNK) <= (q_pos // CHUNK)) & (k_pos < past + dec_seq)
    s = jnp.where(ok, s, NEG_INF)
    p = jnp.exp(s - jnp.max(s, axis=-1, keepdims=True))
    p = p / jnp.sum(p, axis=-1, keepdims=True)
    o = jnp.dot(p.astype(BF16), kcat[:, 0:kv_lora], preferred_element_type=F32)
    o_ref[...] = o.astype(o_ref.dtype).reshape(n_heads, dec_seq, kv_lora)


def _sample_attn(q_lat, cache_ckv, cache_krp, ckv_b, krp_b, row0, dec_seq):
    n_heads, n_s, kw = q_lat.shape
    n_b, past, kv_lora = cache_ckv.shape
    n_keys_pad = -(-(past + dec_seq) // LANES) * LANES
    rb0 = row0 // dec_seq
    return pl.pallas_call(
        functools.partial(_sample_attn_kernel, past=past, dec_seq=dec_seq, kv_lora=kv_lora,
                          n_keys_pad=n_keys_pad),
        grid=(n_b,),
        in_specs=[pl.BlockSpec((n_heads, dec_seq, kw), lambda b: (0, b, 0)),
                  pl.BlockSpec((None, past, kv_lora), lambda b: (b, 0, 0)),
                  pl.BlockSpec((None, past, LANES), lambda b: (b, 0, 0)),
                  pl.BlockSpec((dec_seq, kv_lora), lambda b: (rb0 + b, 0)),
                  pl.BlockSpec((dec_seq, LANES), lambda b: (rb0 + b, 0))],
        out_specs=pl.BlockSpec((n_heads, dec_seq, kv_lora), lambda b: (0, b, 0)),
        out_shape=jax.ShapeDtypeStruct((n_heads, n_s, kv_lora), BF16),
        scratch_shapes=[pltpu.VMEM((n_keys_pad, kw), BF16)],
        compiler_params=_params("parallel"),
        name="sample_attn",
    )(q_lat, cache_ckv, cache_krp, ckv_b, krp_b)


def _o_latent_kernel(o_ref, wuv_ref, out_ref):
    out_ref[...] = jnp.dot(o_ref[...], wuv_ref[...].astype(BF16),
                           preferred_element_type=F32).astype(out_ref.dtype)


def _o_latent(o_lat, w_uv_h):
    n_heads, n_s, kv_lora = o_lat.shape
    return pl.pallas_call(
        _o_latent_kernel,
        grid=(n_heads,),
        in_specs=[pl.BlockSpec((None, n_s, kv_lora), lambda h: (h, 0, 0)),
                  pl.BlockSpec((None, kv_lora, LANES), lambda h: (h, 0, 0))],
        out_specs=pl.BlockSpec((n_s, LANES), lambda h: (0, h)),
        out_shape=jax.ShapeDtypeStruct((n_s, n_heads * LANES), BF16),
        compiler_params=_params("parallel"),
        name="o_latent",
    )(o_lat, w_uv_h)


def _rope_tables(pos, rope_dim):
    half = rope_dim // 2
    inv_freq = ROPE_THETA ** (-jnp.arange(half, dtype=F32) / half)
    ang = pos.astype(F32)[:, None] * inv_freq[None, :]
    cos, sin = jnp.cos(ang), jnp.sin(ang)
    zeros = jnp.zeros((pos.shape[0], LANES - rope_dim), F32)
    return (jnp.concatenate([cos, cos, zeros], axis=1), jnp.concatenate([-sin, sin, zeros], axis=1))


def _mla_mixer(xb, n_prompt, dec_seq, cache_ckv, cache_krope, w_in, q_g, kv_g, w_uq, w_uk, w_uv):
    t = xb.shape[0]
    q_lora = q_g.shape[0]
    kv_lora, n_heads, nope = w_uk.shape
    rope_dim = w_uq.shape[2] - nope
    v_dim = w_uv.shape[2]
    assert nope == LANES and v_dim == LANES and rope_dim * 2 == LANES
    past = cache_ckv.shape[1]
    scale = float((nope + rope_dim) ** -0.5)

    pos = jnp.concatenate([jnp.arange(n_prompt, dtype=jnp.int32),
                           past + jnp.arange(t - n_prompt, dtype=jnp.int32) % dec_seq])
    cos_t, sin_t = _rope_tables(pos, rope_dim)

    hw = -(-(q_lora + kv_lora + LANES) // (2 * LANES)) * (2 * LANES)
    w_in_p = jnp.pad(w_in, ((0, 0), (0, hw - w_in.shape[1])))
    h = _matmul(xb, w_in_p, out_dtype=F32, tn=256, name="mla_in")
    cq, ckv, ckv_b, kr, krp_b = _mla_latent(h, q_g, kv_g, cos_t, sin_t, q_lora, kv_lora, rope_dim)

    w_q = jnp.pad(w_uq, ((0, 0), (0, 0), (0, 2 * LANES - nope - rope_dim))).reshape(q_lora, n_heads * 2 * LANES)
    q = _q_proj(cq, w_q, cos_t, sin_t, scale)

    kn = _matmul(ckv_b[:n_prompt], w_uk.reshape(kv_lora, n_heads * nope), out_dtype=BF16, name="k_nope")
    v = _matmul(ckv_b[:n_prompt], w_uv.reshape(kv_lora, n_heads * v_dim), out_dtype=BF16, name="v_proj")
    attn_p = _prompt_attn(q, kn, krp_b, v, n_prompt, n_heads)

    w_uk_h = jnp.transpose(w_uk, (1, 0, 2))
    w_uv_h = jnp.transpose(w_uv, (1, 0, 2))
    q_lat = _q_latent(q, w_uk_h, n_prompt)
    cache_krp = jnp.pad(cache_krope, ((0, 0), (0, 0), (0, LANES - rope_dim)))
    o_lat = _sample_attn(q_lat, cache_ckv, cache_krp, ckv_b, krp_b, n_prompt, dec_seq)
    attn_s = _o_latent(o_lat, w_uv_h)
    return jnp.concatenate([attn_p, attn_s], axis=0), ckv, kr


def kernel(x_prompt, x_sample, cache_mla_ckv, cache_mla_krope, ln_mix_g, ln_mix_b, ln_ffn_g, ln_ffn_b,
           a_w_in, a_ln_v_g, a_ln_v_b, a_w_s, a_b_s, a_w_out,
           b_w_in, b_q_norm_g, b_kv_norm_g, b_w_uq, b_w_uk, b_w_uv, b_w_out,
           router_w, router_b, exp_w_gu, exp_b_gu, exp_w_down, exp_b_down):
    depth = ln_mix_g.shape[0]
    n_mixers = 2
    alpha = float((2 * depth) ** 0.25)
    bsz, seq, d = x_prompt.shape
    dec_b, dec_seq, _ = x_sample.shape
    n_prompt = bsz * seq
    n_sample = dec_b * dec_seq
    assert bsz == 1 and n_prompt % GMLP_CHUNK == 0 and GMLP_CHUNK % dec_seq == 0

    x = jnp.concatenate([x_prompt.reshape(n_prompt, d), x_sample.reshape(n_sample, d)], axis=0)
    xb = x.astype(BF16)
    v_s, ckv_p, kr_p, ckv_s, kr_s = [], [], [], [], []
    for i in range(depth):
        j = i // n_mixers
        if i % n_mixers == 0:
            u, v = _gmlp_in(xb, a_w_in[j])
            gated, vn = _gmlp_gate(u, v, a_ln_v_g[j], a_ln_v_b[j], a_w_s[j], a_b_s[j], n_prompt, dec_seq)
            pre = _matmul(gated, a_w_out[j], out_dtype=F32, res=x, alpha=alpha, name="gmlp_out")
            v_s.append(vn[n_prompt:].reshape(dec_b, dec_seq, -1))
        else:
            attn, ckv, kr = _mla_mixer(xb, n_prompt, dec_seq, cache_mla_ckv[j], cache_mla_krope[j],
                                       b_w_in[j], b_q_norm_g[j], b_kv_norm_g[j], b_w_uq[j], b_w_uk[j],
                                       b_w_uv[j])
            pre = _matmul(attn, b_w_out[j], out_dtype=F32, res=x, alpha=alpha, name="mla_out")
            ckv_p.append(ckv[:n_prompt].reshape(bsz, seq, -1))
            kr_p.append(kr[:n_prompt].reshape(bsz, seq, -1))
            ckv_s.append(ckv[n_prompt:].reshape(dec_b, dec_seq, -1))
            kr_s.append(kr[n_prompt:].reshape(dec_b, dec_seq, -1))
        x, xb = _moe_layer(pre, ln_mix_g[i], ln_mix_b[i], router_w[i], router_b[i], exp_w_gu, exp_b_gu,
                           exp_w_down, exp_b_down, i, ln_ffn_g[i], ln_ffn_b[i], alpha)

    return (x[:n_prompt].reshape(bsz, seq, d), x[n_prompt:].reshape(dec_b, dec_seq, d),
            jnp.stack(v_s), jnp.stack(ckv_p), jnp.stack(kr_p), jnp.stack(ckv_s), jnp.stack(kr_s))
```

```python
import functools

import numpy as np
import jax
import jax.numpy as jnp
from jax import lax
from jax.experimental import pallas as pl
from jax.experimental.pallas import tpu as pltpu

F32 = jnp.float32
BF16 = jnp.bfloat16

CHUNK = 64
GMLP_CHUNK = 128
GMLP_GROUPS = 8
TOP_K = 4
ROPE_THETA = 10000.0
NEG_INF = -1e30
SWIGLU_LIMIT = 7.0
SWIGLU_ALPHA = 1.702
LN_EPS = 1e-5
RMS_EPS = 1e-6

LANES = 128
MOE_SUPER_ROWS = 1536
MOE_SUB_ROWS = 128
DMA_UNROLL = 8
VMEM_LIMIT = 56 * 1024 * 1024


def _tile(dim, pref):
    if dim <= pref:
        return dim
    t = pref
    while dim % t:
        t //= 2
    return t


def _params(*sem):
    return pltpu.CompilerParams(dimension_semantics=sem, vmem_limit_bytes=VMEM_LIMIT)


def _layer_norm(x, g, b):
    mu = jnp.mean(x, axis=-1, keepdims=True)
    xc = x - mu
    var = jnp.mean(xc * xc, axis=-1, keepdims=True)
    return xc * lax.rsqrt(var + LN_EPS) * g + b


def _rms_norm(x, g):
    return x * lax.rsqrt(jnp.mean(x * x, axis=-1, keepdims=True) + RMS_EPS) * g


def _gelu(x):
    return 0.5 * x * (1.0 + lax.erf(x * np.float32(1.0 / np.sqrt(2.0))))


def _swap_rope_halves(x):
    half = 32
    lane = lax.broadcasted_iota(jnp.int32, x.shape, 1)
    fwd = pltpu.roll(x, LANES - half, 1)
    bwd = pltpu.roll(x, half, 1)
    return jnp.where((lane % (2 * half)) < half, fwd, bwd)


def _mm_kernel(*refs, alpha, has_res):
    if has_res:
        x_ref, w_ref, r_ref, o_ref = refs
    else:
        x_ref, w_ref, o_ref = refs
    acc = jnp.dot(x_ref[...], w_ref[...].astype(BF16), preferred_element_type=F32)
    if has_res:
        acc = alpha * r_ref[...] + acc
    o_ref[...] = acc.astype(o_ref.dtype)


def _matmul(x, w, *, out_dtype, res=None, alpha=1.0, tm=1024, tn=512, name):
    m, k = x.shape
    n = w.shape[1]
    tm, tn = _tile(m, tm), _tile(n, tn)
    in_specs = [pl.BlockSpec((tm, k), lambda i, j: (i, 0)),
                pl.BlockSpec((k, tn), lambda i, j: (0, j))]
    args = [x, w]
    if res is not None:
        in_specs.append(pl.BlockSpec((tm, tn), lambda i, j: (i, j)))
        args.append(res)
    return pl.pallas_call(
        functools.partial(_mm_kernel, alpha=alpha, has_res=res is not None),
        grid=(m // tm, n // tn),
        in_specs=in_specs,
        out_specs=pl.BlockSpec((tm, tn), lambda i, j: (i, j)),
        out_shape=jax.ShapeDtypeStruct((m, n), out_dtype),
        compiler_params=_params("parallel", "arbitrary"),
        name=name,
    )(*args)


def _gmlp_in_kernel(x_ref, wu_ref, wv_ref, u_ref, v_ref):
    x = x_ref[...]
    u_ref[...] = _gelu(jnp.dot(x, wu_ref[...].astype(BF16), preferred_element_type=F32))
    v_ref[...] = _gelu(jnp.dot(x, wv_ref[...].astype(BF16), preferred_element_type=F32))


def _gmlp_in(xb, w_in):
    m, k = xb.shape
    width = w_in.shape[1] // 2
    tm, tn = _tile(m, 1024), _tile(width, 256)
    nj = width // tn
    return pl.pallas_call(
        _gmlp_in_kernel,
        grid=(m // tm, nj),
        in_specs=[pl.BlockSpec((tm, k), lambda i, j: (i, 0)),
                  pl.BlockSpec((k, tn), lambda i, j: (0, j)),
                  pl.BlockSpec((k, tn), lambda i, j: (0, j + nj))],
        out_specs=[pl.BlockSpec((tm, tn), lambda i, j: (i, j)),
                   pl.BlockSpec((tm, tn), lambda i, j: (i, j))],
        out_shape=[jax.ShapeDtypeStruct((m, width), F32)] * 2,
        compiler_params=_params("parallel", "arbitrary"),
        name="gmlp_in",
    )(xb, w_in, w_in)


def _gmlp_gate_kernel(u_ref, v_ref, g_ref, b_ref, ws_ref, bs_ref, o_ref, vn_ref, *, n_prompt_tiles,
                      sample_shift):
    i = pl.program_id(0)
    rows = u_ref.shape[0]
    gdim = u_ref.shape[1] // GMLP_GROUPS
    vn = _layer_norm(v_ref[...], g_ref[...], b_ref[...])
    vn_ref[...] = vn
    vnb = vn.astype(BF16)
    shift = jnp.where(i >= n_prompt_tiles, sample_shift, int(np.log2(GMLP_CHUNK)))
    r = lax.broadcasted_iota(jnp.int32, (rows, rows), 0)
    c = lax.broadcasted_iota(jnp.int32, (rows, rows), 1)
    keep = (c <= r) & ((r >> shift) == (c >> shift))
    bias = bs_ref[...]
    for g in range(GMLP_GROUPS):
        w = jnp.where(keep, ws_ref[g], 0.0).astype(BF16)
        s = jnp.dot(w, vnb[:, g * gdim:(g + 1) * gdim], preferred_element_type=F32)
        s = s + bias[:, g:g + 1]
        o_ref[:, g * gdim:(g + 1) * gdim] = (u_ref[:, g * gdim:(g + 1) * gdim] * s).astype(o_ref.dtype)


def _gmlp_gate(u, v, ln_g, ln_b, w_s, b_s, n_prompt, dec_seq):
    t, width = u.shape
    rows = GMLP_CHUNK
    reps = rows // dec_seq
    w_prompt = w_s
    w_sample = jnp.tile(w_s[:, :dec_seq, :dec_seq], (1, reps, reps))
    b_prompt = b_s.T
    b_sample = jnp.tile(b_s[:, :dec_seq].T, (reps, 1))
    w_all = jnp.stack([w_prompt, w_sample])
    b_all = jnp.stack([b_prompt, b_sample])
    n_prompt_tiles = n_prompt // rows
    sel = lambda i: (i >= n_prompt_tiles).astype(jnp.int32)
    return pl.pallas_call(
        functools.partial(_gmlp_gate_kernel, n_prompt_tiles=n_prompt_tiles,
                          sample_shift=int(np.log2(dec_seq))),
        grid=(t // rows,),
        in_specs=[pl.BlockSpec((rows, width), lambda i: (i, 0)),
                  pl.BlockSpec((rows, width), lambda i: (i, 0)),
                  pl.BlockSpec((1, width), lambda i: (0, 0)),
                  pl.BlockSpec((1, width), lambda i: (0, 0)),
                  pl.BlockSpec((None, GMLP_GROUPS, rows, rows), lambda i: (sel(i), 0, 0, 0)),
                  pl.BlockSpec((None, rows, GMLP_GROUPS), lambda i: (sel(i), 0, 0))],
        out_specs=[pl.BlockSpec((rows, width), lambda i: (i, 0)),
                   pl.BlockSpec((rows, width), lambda i: (i, 0))],
        out_shape=[jax.ShapeDtypeStruct((t, width), BF16), jax.ShapeDtypeStruct((t, width), F32)],
        compiler_params=_params("parallel"),
        name="gmlp_gate",
    )(u, v, ln_g.reshape(1, -1), ln_b.reshape(1, -1), w_all, b_all)


def _ln_router_kernel(p_ref, g_ref, b_ref, rw_ref, rb_ref, y_ref, pk_ref, idx_ref, gate_ref):
    y = _layer_norm(p_ref[...], g_ref[...], b_ref[...])
    y_ref[...] = y
    rw = rw_ref[...]
    y_hi, w_hi = y.astype(BF16), rw.astype(BF16)
    half = y.shape[1] // 2
    y_bits = lax.bitcast_convert_type(y_hi.astype(F32), jnp.uint32)
    pk_ref[...] = (y_bits[:, :half] >> 16) | (y_bits[:, half:] & jnp.uint32(0xFFFF0000))
    y_lo = (y - y_hi.astype(F32)).astype(BF16)
    w_lo = (rw - w_hi.astype(F32)).astype(BF16)
    logits = (jnp.dot(y_hi, w_hi, preferred_element_type=F32)
              + (jnp.dot(y_lo, w_hi, preferred_element_type=F32)
                 + jnp.dot(y_hi, w_lo, preferred_element_type=F32))) + rb_ref[...]
    n_e = logits.shape[1]
    col = lax.broadcasted_iota(jnp.int32, logits.shape, 1)
    vals, idxs = [], []
    cur = logits
    for _ in range(TOP_K):
        mx = jnp.max(cur, axis=-1, keepdims=True)
        am = jnp.min(jnp.where(cur == mx, col, n_e), axis=-1, keepdims=True)
        vals.append(mx)
        idxs.append(am)
        cur = jnp.where(col == am, -jnp.inf, cur)
    es = [jnp.exp(v - vals[0]) for v in vals]
    denom = es[0]
    for e in es[1:]:
        denom = denom + e
    kcol = lax.broadcasted_iota(jnp.int32, (logits.shape[0], TOP_K), 1)
    idx_out = jnp.zeros((logits.shape[0], TOP_K), jnp.int32)
    gate_out = jnp.zeros((logits.shape[0], TOP_K), F32)
    for k in range(TOP_K):
        idx_out = jnp.where(kcol == k, idxs[k], idx_out)
        gate_out = jnp.where(kcol == k, es[k] / denom, gate_out)
    idx_ref[...] = idx_out
    gate_ref[...] = gate_out


def _ln_router(pre, g, b, rw, rb):
    t, d = pre.shape
    n_e = rw.shape[1]
    tm = _tile(t, 256)
    return pl.pallas_call(
        _ln_router_kernel,
        grid=(t // tm,),
        in_specs=[pl.BlockSpec((tm, d), lambda i: (i, 0)),
                  pl.BlockSpec((1, d), lambda i: (0, 0)),
                  pl.BlockSpec((1, d), lambda i: (0, 0)),
                  pl.BlockSpec((d, n_e), lambda i: (0, 0)),
                  pl.BlockSpec((1, n_e), lambda i: (0, 0))],
        out_specs=[pl.BlockSpec((tm, d), lambda i: (i, 0)),
                   pl.BlockSpec((tm, d // 2), lambda i: (i, 0)),
                   pl.BlockSpec((tm, TOP_K), lambda i: (i, 0)),
                   pl.BlockSpec((tm, TOP_K), lambda i: (i, 0))],
        out_shape=[jax.ShapeDtypeStruct((t, d), F32),
                   jax.ShapeDtypeStruct((t, d // 2), jnp.uint32),
                   jax.ShapeDtypeStruct((t, TOP_K), jnp.int32),
                   jax.ShapeDtypeStruct((t, TOP_K), F32)],
        compiler_params=_params("parallel"),
        name="ln_router",
    )(pre, g.reshape(1, -1), b.reshape(1, -1), rw, rb.reshape(1, -1))


def _route_tables(top_idx, n_e, n_sb):
    t = top_idx.shape[0]
    flat_e = top_idx.reshape(-1)
    onehot = (flat_e[:, None] == jnp.arange(n_e, dtype=jnp.int32)[None, :]).astype(jnp.int32)
    csum = jnp.cumsum(onehot, axis=0)
    rank = jnp.take_along_axis(csum, flat_e[:, None], axis=1)[:, 0] - 1
    counts = csum[-1]
    sb_per_e = (counts + MOE_SUPER_ROWS - 1) // MOE_SUPER_ROWS
    per_sb = -(-counts // jnp.maximum(sb_per_e, 1))
    per_sb = jnp.maximum((per_sb + MOE_SUB_ROWS - 1) // MOE_SUB_ROWS * MOE_SUB_ROWS, MOE_SUB_ROWS)
    sb_end = jnp.cumsum(sb_per_e)
    sb_first = sb_end - sb_per_e
    n_used = sb_end[-1]
    sb = jnp.arange(n_sb, dtype=jnp.int32)
    sb_c = jnp.minimum(sb, n_used - 1)
    e_of = jnp.minimum(jnp.searchsorted(sb_end, sb_c, side="right"), n_e - 1).astype(jnp.int32)
    rows = jnp.clip(counts[e_of] - (sb_c - sb_first[e_of]) * per_sb[e_of], 0, per_sb[e_of])
    rows = jnp.where(sb < n_used, rows, 0).astype(jnp.int32)
    a_sb = sb_first[flat_e] + rank // per_sb[flat_e]
    dest = (a_sb * MOE_SUPER_ROWS + rank % per_sb[flat_e]).astype(jnp.int32)
    tok = (jnp.arange(t * TOP_K, dtype=jnp.int32) // TOP_K)
    slot_tok = jnp.zeros((n_sb * MOE_SUPER_ROWS,), jnp.int32).at[dest].set(tok)
    return e_of, rows, slot_tok, dest


def _row_blocks(rows, compute, zero):
    sub = MOE_SUB_ROWS
    nb = (rows + sub - 1) // sub
    n_big = nb // 4
    rem = nb % 4

    def big(s, c):
        compute(pl.multiple_of(s * 4 * sub, 4 * sub), 4 * sub)
        return c
    lax.fori_loop(0, n_big, big, 0)

    @pl.when(rem >= 2)
    def _():
        compute(pl.multiple_of(n_big * 4 * sub, 2 * sub), 2 * sub)

    @pl.when(rem % 2 == 1)
    def _():
        compute(pl.multiple_of((nb - 1) * sub, sub), sub)

    def zero_body(s, c):
        zero(pl.multiple_of(s * sub, sub))
        return c
    lax.fori_loop(nb, MOE_SUPER_ROWS // sub, zero_body, 0)


def _moe_up_kernel(sb_e, sb_rows, slot_tok, y_hbm, wg_ref, wu_ref, bg_ref, bu_ref, act_ref,
                   xpk, xb16, wb16, sem, *, n_sb, nj):
    sb = pl.program_id(0)
    j = pl.program_id(1)
    tn = wg_ref.shape[1]
    sub = MOE_SUB_ROWS
    per_sub = sub // nj
    rows = sb_rows[sb]

    def padded(r, m):
        return (r + m - 1) // m * m

    p_cur = padded(rows, sub)
    base_next = jnp.minimum(sb + 1, n_sb - 1) * MOE_SUPER_ROWS

    def row_copy(tok, r):
        return pltpu.make_async_copy(y_hbm.at[pl.ds(tok, 1)], xpk.at[pl.ds(r, 1)], sem.at[0])

    def wait_rows(n):
        n = pl.multiple_of(n, DMA_UNROLL)
        pltpu.make_async_copy(y_hbm.at[pl.ds(0, n)], xpk.at[pl.ds(0, n)], sem.at[0]).wait()

    def fetch_loop(sb_i, lo, hi):
        base = sb_i * MOE_SUPER_ROWS

        def body_grp(g, c):
            r0 = g * DMA_UNROLL
            for u in range(DMA_UNROLL):
                row_copy(slot_tok[base + r0 + u], r0 + u).start()
            return c
        lax.fori_loop(lo // DMA_UNROLL, hi // DMA_UNROLL, body_grp, 0)

    @pl.when((sb == 0) & (j == 0))
    def _():
        xpk[...] = jnp.zeros_like(xpk)
        fetch_loop(0, 0, padded(sb_rows[0], DMA_UNROLL))

    @pl.when(j == 0)
    def _():
        p_prev = jnp.where(sb > 0, padded(sb_rows[jnp.maximum(sb - 1, 0)], sub), 0)
        n_wait = jnp.maximum(p_prev, padded(rows, DMA_UNROLL))

        @pl.when(n_wait > 0)
        def _():
            wait_rows(n_wait)
        half = xpk.shape[1]
        pk = xpk[...]
        xb16[:, :half] = lax.bitcast_convert_type(pk << 16, F32).astype(BF16)
        xb16[:, half:] = lax.bitcast_convert_type(pk & jnp.uint32(0xFFFF0000), F32).astype(BF16)

        @pl.when(sb + 1 < n_sb)
        def _():
            fetch_loop(sb + 1, p_cur, padded(sb_rows[sb + 1], DMA_UNROLL))

    @pl.when(rows > 0)
    def _():
        wb16[:, :tn] = wg_ref[...].astype(BF16)
        wb16[:, tn:] = wu_ref[...].astype(BF16)

    def compute(r0, n):
        k0 = j * (p_cur // nj) + (r0 // sub) * per_sub
        for u in range((n // sub) * per_sub):
            row_copy(slot_tok[base_next + k0 + u], k0 + u).start()
        h = jnp.dot(xb16[pl.ds(r0, n), :], wb16[...], preferred_element_type=F32)
        gate = jnp.minimum(h[:, :tn] + bg_ref[...], SWIGLU_LIMIT)
        up = jnp.clip(h[:, tn:] + bu_ref[...], -SWIGLU_LIMIT, SWIGLU_LIMIT)
        a = (up + 1.0) * gate * jax.nn.sigmoid(SWIGLU_ALPHA * gate)
        act_ref[pl.ds(r0, n), :] = a.astype(act_ref.dtype)

    def zero(r0):
        act_ref[pl.ds(r0, MOE_SUB_ROWS), :] = jnp.zeros((MOE_SUB_ROWS, tn), act_ref.dtype)

    _row_blocks(rows, compute, zero)

    @pl.when((sb == n_sb - 1) & (j == nj - 1) & (p_cur > 0))
    def _():
        wait_rows(p_cur)


def _moe_up(y_pk, w_gu, b_gu, layer, sb_e, sb_rows, slot_tok):
    _, n_e, d, ff2 = w_gu.shape
    assert y_pk.shape[1] * 2 == d
    ff = ff2 // 2
    n_sb = sb_e.shape[0]
    tn = _tile(ff, 256)
    nj = ff // tn
    assert MOE_SUB_ROWS % nj == 0 and MOE_SUPER_ROWS % (4 * MOE_SUB_ROWS) == 0
    b4 = b_gu.reshape(b_gu.shape[0], n_e, 1, ff2)

    def jw(sb, j, rows):
        return jnp.where(rows[sb] > 0, j, nj - 1)

    grid_spec = pltpu.PrefetchScalarGridSpec(
        num_scalar_prefetch=3,
        grid=(n_sb, nj),
        in_specs=[pl.BlockSpec(memory_space=pl.ANY),
                  pl.BlockSpec((None, None, d, tn), lambda sb, j, e, rows, st: (layer, e[sb], 0, jw(sb, j, rows))),
                  pl.BlockSpec((None, None, d, tn), lambda sb, j, e, rows, st: (layer, e[sb], 0, jw(sb, j, rows) + nj)),
                  pl.BlockSpec((None, None, 1, tn), lambda sb, j, e, rows, st: (layer, e[sb], 0, jw(sb, j, rows))),
                  pl.BlockSpec((None, None, 1, tn), lambda sb, j, e, rows, st: (layer, e[sb], 0, jw(sb, j, rows) + nj))],
        out_specs=pl.BlockSpec((MOE_SUPER_ROWS, tn), lambda sb, j, e, rows, st: (sb, j)),
        scratch_shapes=[pltpu.VMEM((MOE_SUPER_ROWS, d // 2), jnp.uint32),
                        pltpu.VMEM((MOE_SUPER_ROWS, d), BF16),
                        pltpu.VMEM((d, 2 * tn), BF16),
                        pltpu.SemaphoreType.DMA((1,))],
    )
    return pl.pallas_call(
        functools.partial(_moe_up_kernel, n_sb=n_sb, nj=nj),
        grid_spec=grid_spec,
        out_shape=jax.ShapeDtypeStruct((n_sb * MOE_SUPER_ROWS, ff), BF16),
        compiler_params=_params("arbitrary", "arbitrary"),
        name="moe_up",
    )(sb_e, sb_rows, slot_tok, y_pk, w_gu, w_gu, b4, b4)


def _moe_down_kernel(sb_e, sb_rows, a_ref, w_ref, b_ref, o_ref, wb16):
    sb = pl.program_id(0)
    tn = w_ref.shape[1]
    rows = sb_rows[sb]

    @pl.when(rows > 0)
    def _():
        wb16[...] = w_ref[...].astype(BF16)

    def compute(r0, n):
        o_ref[pl.ds(r0, n), :] = jnp.dot(a_ref[pl.ds(r0, n), :], wb16[...],
                                         preferred_element_type=F32) + b_ref[...]

    def zero(r0):
        o_ref[pl.ds(r0, MOE_SUB_ROWS), :] = jnp.zeros((MOE_SUB_ROWS, tn), F32)

    _row_blocks(rows, compute, zero)


def _moe_down(act, w_down, b_down, layer, sb_e, sb_rows):
    n_slots, ff = act.shape
    _, n_e, _, d = w_down.shape
    n_sb = sb_e.shape[0]
    tn = _tile(d, 1024)
    nj = d // tn
    b4 = b_down.reshape(b_down.shape[0], n_e, 1, d)

    def jw(sb, j, rows):
        return jnp.where(rows[sb] > 0, j, nj - 1)

    grid_spec = pltpu.PrefetchScalarGridSpec(
        num_scalar_prefetch=2,
        grid=(n_sb, nj),
        in_specs=[pl.BlockSpec((MOE_SUPER_ROWS, ff), lambda sb, j, e, rows: (sb, 0)),
                  pl.BlockSpec((None, None, ff, tn), lambda sb, j, e, rows: (layer, e[sb], 0, jw(sb, j, rows))),
                  pl.BlockSpec((None, None, 1, tn), lambda sb, j, e, rows: (layer, e[sb], 0, jw(sb, j, rows)))],
        out_specs=pl.BlockSpec((MOE_SUPER_ROWS, tn), lambda sb, j, e, rows: (sb, j)),
        scratch_shapes=[pltpu.VMEM((ff, tn), BF16)],
    )
    return pl.pallas_call(
        _moe_down_kernel,
        grid_spec=grid_spec,
        out_shape=jax.ShapeDtypeStruct((n_slots, d), F32),
        compiler_params=_params("arbitrary", "arbitrary"),
        name="moe_down",
    )(sb_e, sb_rows, act, w_down, b4)


def _moe_combine_kernel(dest, ys_hbm, gate_ref, y_ref, g_ref, b_ref, o_ref, ob_ref, gbuf, sem, *, alpha):
    i = pl.program_id(0)
    n = pl.num_programs(0)
    tm = y_ref.shape[0]
    slot = i % 2

    def issue(tile, s):
        base = tile * tm * TOP_K

        def body(r, c):
            for k in range(TOP_K):
                pltpu.make_async_copy(ys_hbm.at[pl.ds(dest[base + r * TOP_K + k], 1)],
                                      gbuf.at[s, k, pl.ds(r, 1)], sem.at[s]).start()
            return c
        lax.fori_loop(0, tm, body, 0, unroll=DMA_UNROLL // TOP_K)

    @pl.when(i == 0)
    def _():
        issue(0, 0)

    @pl.when(i + 1 < n)
    def _():
        issue(i + 1, 1 - slot)

    for k in range(TOP_K):
        pltpu.make_async_copy(ys_hbm.at[pl.ds(0, tm)], gbuf.at[slot, k], sem.at[slot]).wait()

    gates = gate_ref[...]
    moe = gates[:, 0:1] * gbuf[slot, 0]
    for k in range(1, TOP_K):
        moe = moe + gates[:, k:k + 1] * gbuf[slot, k]
    out = _layer_norm(alpha * y_ref[...] + moe, g_ref[...], b_ref[...])
    o_ref[...] = out
    ob_ref[...] = out.astype(ob_ref.dtype)


def _moe_combine(ys, dest, gates, y, g, b, alpha):
    t, d = y.shape
    tm = _tile(t, 128)
    grid_spec = pltpu.PrefetchScalarGridSpec(
        num_scalar_prefetch=1,
        grid=(t // tm,),
        in_specs=[pl.BlockSpec(memory_space=pl.ANY),
                  pl.BlockSpec((tm, TOP_K), lambda i, dst: (i, 0)),
                  pl.BlockSpec((tm, d), lambda i, dst: (i, 0)),
                  pl.BlockSpec((1, d), lambda i, dst: (0, 0)),
                  pl.BlockSpec((1, d), lambda i, dst: (0, 0))],
        out_specs=[pl.BlockSpec((tm, d), lambda i, dst: (i, 0)),
                   pl.BlockSpec((tm, d), lambda i, dst: (i, 0))],
        scratch_shapes=[pltpu.VMEM((2, TOP_K, tm, d), F32),
                        pltpu.SemaphoreType.DMA((2,))],
    )
    return pl.pallas_call(
        functools.partial(_moe_combine_kernel, alpha=alpha),
        grid_spec=grid_spec,
        out_shape=[jax.ShapeDtypeStruct((t, d), F32), jax.ShapeDtypeStruct((t, d), BF16)],
        compiler_params=_params("arbitrary"),
        name="moe_combine",
    )(dest, ys, gates, y, g.reshape(1, -1), b.reshape(1, -1))


def _moe_layer(pre, ln_g, ln_b, rw, rb, w_gu, b_gu, w_down, b_down, layer, ffn_g, ffn_b, alpha):
    t = pre.shape[0]
    n_e = rw.shape[1]
    y, y_pk, top_idx, gates = _ln_router(pre, ln_g, ln_b, rw, rb)
    n_sb = (t * TOP_K) // MOE_SUPER_ROWS + n_e
    sb_e, sb_rows, slot_tok, dest = _route_tables(top_idx, n_e, n_sb)
    act = _moe_up(y_pk, w_gu, b_gu, layer, sb_e, sb_rows, slot_tok)
    ys = _moe_down(act, w_down, b_down, layer, sb_e, sb_rows)
    return _moe_combine(ys, dest, gates, y, ffn_g, ffn_b, alpha)


def _mla_latent_kernel(h_ref, qg_ref, kvg_ref, cos_ref, sin_ref, cq_ref, ckv_ref, ckvb_ref, kr_ref,
                       krp_ref, *, q_lora, kv_lora, rope_dim):
    h = h_ref[...]
    cq_ref[...] = _rms_norm(h[:, :q_lora], qg_ref[...]).astype(cq_ref.dtype)
    ckv = _rms_norm(h[:, q_lora:q_lora + kv_lora], kvg_ref[...])
    ckv_ref[...] = ckv
    ckvb_ref[...] = ckv.astype(ckvb_ref.dtype)
    x = h[:, q_lora + kv_lora:q_lora + kv_lora + LANES]
    kr = x * cos_ref[...] + _swap_rope_halves(x) * sin_ref[...]
    kr_ref[...] = kr[:, :rope_dim]
    krp_ref[...] = kr.astype(krp_ref.dtype)


def _mla_latent(h, q_g, kv_g, cos_t, sin_t, q_lora, kv_lora, rope_dim):
    t, hw = h.shape
    tm = _tile(t, 512)
    return pl.pallas_call(
        functools.partial(_mla_latent_kernel, q_lora=q_lora, kv_lora=kv_lora, rope_dim=rope_dim),
        grid=(t // tm,),
        in_specs=[pl.BlockSpec((tm, hw), lambda i: (i, 0)),
                  pl.BlockSpec((1, q_lora), lambda i: (0, 0)),
                  pl.BlockSpec((1, kv_lora), lambda i: (0, 0)),
                  pl.BlockSpec((tm, LANES), lambda i: (i, 0)),
                  pl.BlockSpec((tm, LANES), lambda i: (i, 0))],
        out_specs=[pl.BlockSpec((tm, q_lora), lambda i: (i, 0)),
                   pl.BlockSpec((tm, kv_lora), lambda i: (i, 0)),
                   pl.BlockSpec((tm, kv_lora), lambda i: (i, 0)),
                   pl.BlockSpec((tm, rope_dim), lambda i: (i, 0)),
                   pl.BlockSpec((tm, LANES), lambda i: (i, 0))],
        out_shape=[jax.ShapeDtypeStruct((t, q_lora), BF16),
                   jax.ShapeDtypeStruct((t, kv_lora), F32),
                   jax.ShapeDtypeStruct((t, kv_lora), BF16),
                   jax.ShapeDtypeStruct((t, rope_dim), F32),
                   jax.ShapeDtypeStruct((t, LANES), BF16)],
        compiler_params=_params("parallel"),
        name="mla_latent",
    )(h, q_g.reshape(1, -1), kv_g.reshape(1, -1), cos_t, sin_t)


def _q_proj_kernel(x_ref, w_ref, cos_ref, sin_ref, o_ref, *, scale):
    acc = jnp.dot(x_ref[...], w_ref[...].astype(BF16), preferred_element_type=F32) * scale
    cos, sin = cos_ref[...], sin_ref[...]
    for g in range(acc.shape[1] // LANES):
        blk = acc[:, g * LANES:(g + 1) * LANES]
        if g % 2:
            blk = blk * cos + _swap_rope_halves(blk) * sin
        o_ref[:, g * LANES:(g + 1) * LANES] = blk.astype(o_ref.dtype)


def _q_proj(cq, w_q, cos_t, sin_t, scale):
    t, k = cq.shape
    n = w_q.shape[1]
    tm, tn = _tile(t, 1024), _tile(n, 512)
    return pl.pallas_call(
        functools.partial(_q_proj_kernel, scale=scale),
        grid=(t // tm, n // tn),
        in_specs=[pl.BlockSpec((tm, k), lambda i, j: (i, 0)),
                  pl.BlockSpec((k, tn), lambda i, j: (0, j)),
                  pl.BlockSpec((tm, LANES), lambda i, j: (i, 0)),
                  pl.BlockSpec((tm, LANES), lambda i, j: (i, 0))],
        out_specs=pl.BlockSpec((tm, tn), lambda i, j: (i, j)),
        out_shape=jax.ShapeDtypeStruct((t, n), BF16),
        compiler_params=_params("parallel", "arbitrary"),
        name="q_proj",
    )(cq, w_q, cos_t, sin_t)


def _prompt_attn_kernel(q_ref, kn_ref, kr_ref, v_ref, o_ref, kcat, vcat, s_e0, s_e1, s_o0, s_o1,
                        m_0, m_1, acc_0, acc_1, *, tq, tk):
    qi = pl.program_id(1)
    hd = 2 * LANES
    s_even, s_odd = (s_e0, s_e1), (s_o0, s_o1)
    m_refs, acc_refs = (m_0, m_1), (acc_0, acc_1)

    @pl.when(qi == 0)
    def _():
        for a in range(2):
            kcat[a, :, :LANES] = kn_ref[:, a * LANES:(a + 1) * LANES]
            kcat[a, :, LANES:] = kr_ref[...]
            vcat[a, :, :LANES] = v_ref[:, a * LANES:(a + 1) * LANES]
            vcat[a, :, LANES:] = jnp.ones((vcat.shape[1], LANES), vcat.dtype)

    for a in range(2):
        m_refs[a][...] = jnp.full(m_refs[a].shape, NEG_INF, F32)
        acc_refs[a][...] = jnp.zeros(acc_refs[a].shape, F32)

    def scores(kb, bufs, r_lo=0):
        k0 = pl.multiple_of(kb * tk, tk)
        for a in range(2):
            bufs[a][r_lo:, :] = lax.dot_general(q_ref[r_lo:, a * hd:(a + 1) * hd], kcat[a, pl.ds(k0, tk), :],
                                                (((1,), (1,)), ((), ())), preferred_element_type=F32)

    def softmax_pv(kb, bufs, masked, r_lo=0):
        k0 = pl.multiple_of(kb * tk, tk)
        if masked:
            r = qi * tq + r_lo + lax.broadcasted_iota(jnp.int32, (tq - r_lo, tk), 0)
            c = k0 + lax.broadcasted_iota(jnp.int32, (tq - r_lo, tk), 1)
            ok = (c // CHUNK) <= (r // CHUNK)
        for a in range(2):
            s = bufs[a][r_lo:, :]
            if masked:
                s = jnp.where(ok, s, NEG_INF)
            m_prev = m_refs[a][r_lo:, :]
            m_new = jnp.maximum(m_prev, jnp.max(s, axis=-1, keepdims=True))
            p = jnp.exp(s - m_new).astype(BF16)
            acc_refs[a][r_lo:, :] = jnp.exp(m_prev - m_new) * acc_refs[a][r_lo:, :] + jnp.dot(
                p, vcat[a, pl.ds(k0, tk), :], preferred_element_type=F32)
            m_refs[a][r_lo:, :] = m_new

    n_pair = (qi * tq) // (2 * tk)
    scores(0, s_even)

    def body(i, carry):
        kb = 2 * i
        scores(kb + 1, s_odd)
        softmax_pv(kb, s_even, False)
        scores(kb + 2, s_even)
        softmax_pv(kb + 1, s_odd, False)
        return carry
    lax.fori_loop(0, n_pair, body, 0)
    kb = 2 * n_pair
    scores(kb + 1, s_odd, tk)
    softmax_pv(kb, s_even, True)
    softmax_pv(kb + 1, s_odd, True, tk)
    for a in range(2):
        o_ref[:, a * LANES:(a + 1) * LANES] = (
            acc_refs[a][:, :LANES] / acc_refs[a][:, LANES:]).astype(o_ref.dtype)


def _prompt_attn(q, kn, krp, v, n_prompt, n_heads):
    tq = _tile(n_prompt, 1024)
    tk = tq // 2
    assert tk % CHUNK == 0 and n_prompt % tq == 0
    return pl.pallas_call(
        functools.partial(_prompt_attn_kernel, tq=tq, tk=tk),
        grid=(n_heads // 2, n_prompt // tq),
        in_specs=[pl.BlockSpec((tq, 4 * LANES), lambda p, i: (i, p)),
                  pl.BlockSpec((n_prompt, 2 * LANES), lambda p, i: (0, p), pipeline_mode=pl.Buffered(1)),
                  pl.BlockSpec((n_prompt, LANES), lambda p, i: (0, 0), pipeline_mode=pl.Buffered(1)),
                  pl.BlockSpec((n_prompt, 2 * LANES), lambda p, i: (0, p), pipeline_mode=pl.Buffered(1))],
        out_specs=pl.BlockSpec((tq, 2 * LANES), lambda p, i: (i, p)),
        out_shape=jax.ShapeDtypeStruct((n_prompt, n_heads * LANES), BF16),
        scratch_shapes=([pltpu.VMEM((2, n_prompt, 2 * LANES), BF16)] * 2
                        + [pltpu.VMEM((tq, tk), F32)] * 4
                        + [pltpu.VMEM((tq, 1), F32)] * 2
                        + [pltpu.VMEM((tq, 2 * LANES), F32)] * 2),
        compiler_params=_params("arbitrary", "arbitrary"),
        name="prompt_attn",
    )(q, kn, krp, v)


def _q_latent_kernel(q_ref, wuk_ref, o_ref, *, kv_lora):
    qn = q_ref[:, :LANES]
    q_lat = lax.dot_general(qn, wuk_ref[...].astype(BF16), (((1,), (1,)), ((), ())),
                            preferred_element_type=F32)
    o_ref[:, :kv_lora] = q_lat.astype(o_ref.dtype)
    o_ref[:, kv_lora:] = q_ref[:, LANES:]


def _q_latent(q, w_uk_h, n_prompt):
    n_heads, kv_lora, _ = w_uk_h.shape
    n_s = q.shape[0] - n_prompt
    assert n_prompt % n_s == 0
    rb = n_prompt // n_s
    return pl.pallas_call(
        functools.partial(_q_latent_kernel, kv_lora=kv_lora),
        grid=(n_heads,),
        in_specs=[pl.BlockSpec((n_s, 2 * LANES), lambda h: (rb, h)),
                  pl.BlockSpec((None, kv_lora, LANES), lambda h: (h, 0, 0))],
        out_specs=pl.BlockSpec((None, n_s, kv_lora + LANES), lambda h: (h, 0, 0)),
        out_shape=jax.ShapeDtypeStruct((n_heads, n_s, kv_lora + LANES), BF16),
        compiler_params=_params("parallel"),
        name="q_latent",
    )(q, w_uk_h)


def _sample_attn_kernel(q_ref, cc_ref, ck_ref, nc_ref, nk_ref, o_ref, kcat, *, past, dec_seq, kv_lora,
                        n_keys_pad):
    n_heads = q_ref.shape[0]
    n_q = n_heads * dec_seq
    kw = kv_lora + LANES
    kcat[0:past, 0:kv_lora] = cc_ref[...].astype(BF16)
    kcat[0:past, kv_lora:kw] = ck_ref[...].astype(BF16)
    kcat[past:past + dec_seq, 0:kv_lora] = nc_ref[...]
    kcat[past:past + dec_seq, kv_lora:kw] = nk_ref[...]
    kcat[past + dec_seq:n_keys_pad, :] = jnp.zeros((n_keys_pad - past - dec_seq, kw), BF16)
    q = q_ref[...].reshape(n_q, kw)
    s = lax.dot_general(q, kcat[...], (((1,), (1,)), ((), ())), preferred_element_type=F32)
    q_pos = past + lax.broadcasted_iota(jnp.int32, s.shape, 0) % dec_seq
    k_pos = lax.broadcasted_iota(jnp.int32, s.shape, 1)
    ok = ((k_pos // CHUNK) <= (q_pos // CHUNK)) & (k_pos < past + dec_seq)
    s = jnp.where(ok, s, NEG_INF)
    p = jnp.exp(s - jnp.max(s, axis=-1, keepdims=True))
    p = p / jnp.sum(p, axis=-1, keepdims=True)
    o = jnp.dot(p.astype(BF16), kcat[:, 0:kv_lora], preferred_element_type=F32)
    o_ref[...] = o.astype(o_ref.dtype).reshape(n_heads, dec_seq, kv_lora)


def _sample_attn(q_lat, cache_ckv, cache_krp, ckv_b, krp_b, row0, dec_seq):
    n_heads, n_s, kw = q_lat.shape
    n_b, past, kv_lora = cache_ckv.shape
    n_keys_pad = -(-(past + dec_seq) // LANES) * LANES
    rb0 = row0 // dec_seq
    return pl.pallas_call(
        functools.partial(_sample_attn_kernel, past=past, dec_seq=dec_seq, kv_lora=kv_lora,
                          n_keys_pad=n_keys_pad),
        grid=(n_b,),
        in_specs=[pl.BlockSpec((n_heads, dec_seq, kw), lambda b: (0, b, 0)),
                  pl.BlockSpec((None, past, kv_lora), lambda b: (b, 0, 0)),
                  pl.BlockSpec((None, past, LANES), lambda b: (b, 0, 0)),
                  pl.BlockSpec((dec_seq, kv_lora), lambda b: (rb0 + b, 0)),
                  pl.BlockSpec((dec_seq, LANES), lambda b: (rb0 + b, 0))],
        out_specs=pl.BlockSpec((n_heads, dec_seq, kv_lora), lambda b: (0, b, 0)),
        out_shape=jax.ShapeDtypeStruct((n_heads, n_s, kv_lora), BF16),
        scratch_shapes=[pltpu.VMEM((n_keys_pad, kw), BF16)],
        compiler_params=_params("parallel"),
        name="sample_attn",
    )(q_lat, cache_ckv, cache_krp, ckv_b, krp_b)


def _o_latent_kernel(o_ref, wuv_ref, out_ref):
    out_ref[...] = jnp.dot(o_ref[...], wuv_ref[...].astype(BF16),
                           preferred_element_type=F32).astype(out_ref.dtype)


def _o_latent(o_lat, w_uv_h):
    n_heads, n_s, kv_lora = o_lat.shape
    return pl.pallas_call(
        _o_latent_kernel,
        grid=(n_heads,),
        in_specs=[pl.BlockSpec((None, n_s, kv_lora), lambda h: (h, 0, 0)),
                  pl.BlockSpec((None, kv_lora, LANES), lambda h: (h, 0, 0))],
        out_specs=pl.BlockSpec((n_s, LANES), lambda h: (0, h)),
        out_shape=jax.ShapeDtypeStruct((n_s, n_heads * LANES), BF16),
        compiler_params=_params("parallel"),
        name="o_latent",
    )(o_lat, w_uv_h)


def _rope_tables(pos, rope_dim):
    half = rope_dim // 2
    inv_freq = ROPE_THETA ** (-jnp.arange(half, dtype=F32) / half)
    ang = pos.astype(F32)[:, None] * inv_freq[None, :]
    cos, sin = jnp.cos(ang), jnp.sin(ang)
    zeros = jnp.zeros((pos.shape[0], LANES - rope_dim), F32)
    return (jnp.concatenate([cos, cos, zeros], axis=1), jnp.concatenate([-sin, sin, zeros], axis=1))


def _mla_mixer(xb, n_prompt, dec_seq, cache_ckv, cache_krope, w_in, q_g, kv_g, w_uq, w_uk, w_uv):
    t = xb.shape[0]
    q_lora = q_g.shape[0]
    kv_lora, n_heads, nope = w_uk.shape
    rope_dim = w_uq.shape[2] - nope
    v_dim = w_uv.shape[2]
    assert nope == LANES and v_dim == LANES and rope_dim * 2 == LANES
    past = cache_ckv.shape[1]
    scale = float((nope + rope_dim) ** -0.5)

    pos = jnp.concatenate([jnp.arange(n_prompt, dtype=jnp.int32),
                           past + jnp.arange(t - n_prompt, dtype=jnp.int32) % dec_seq])
    cos_t, sin_t = _rope_tables(pos, rope_dim)

    hw = -(-(q_lora + kv_lora + LANES) // (2 * LANES)) * (2 * LANES)
    w_in_p = jnp.pad(w_in, ((0, 0), (0, hw - w_in.shape[1])))
    h = _matmul(xb, w_in_p, out_dtype=F32, tn=256, name="mla_in")
    cq, ckv, ckv_b, kr, krp_b = _mla_latent(h, q_g, kv_g, cos_t, sin_t, q_lora, kv_lora, rope_dim)

    w_q = jnp.pad(w_uq, ((0, 0), (0, 0), (0, 2 * LANES - nope - rope_dim))).reshape(q_lora, n_heads * 2 * LANES)
    q = _q_proj(cq, w_q, cos_t, sin_t, scale)

    kn = _matmul(ckv_b[:n_prompt], w_uk.reshape(kv_lora, n_heads * nope), out_dtype=BF16, name="k_nope")
    v = _matmul(ckv_b[:n_prompt], w_uv.reshape(kv_lora, n_heads * v_dim), out_dtype=BF16, name="v_proj")
    attn_p = _prompt_attn(q, kn, krp_b, v, n_prompt, n_heads)

    w_uk_h = jnp.transpose(w_uk, (1, 0, 2))
    w_uv_h = jnp.transpose(w_uv, (1, 0, 2))
    q_lat = _q_latent(q, w_uk_h, n_prompt)
    cache_krp = jnp.pad(cache_krope, ((0, 0), (0, 0), (0, LANES - rope_dim)))
    o_lat = _sample_attn(q_lat, cache_ckv, cache_krp, ckv_b, krp_b, n_prompt, dec_seq)
    attn_s = _o_latent(o_lat, w_uv_h)
    return jnp.concatenate([attn_p, attn_s], axis=0), ckv, kr


def kernel(x_prompt, x_sample, cache_mla_ckv, cache_mla_krope, ln_mix_g, ln_mix_b, ln_ffn_g, ln_ffn_b,
           a_w_in, a_ln_v_g, a_ln_v_b, a_w_s, a_b_s, a_w_out,
           b_w_in, b_q_norm_g, b_kv_norm_g, b_w_uq, b_w_uk, b_w_uv, b_w_out,
           router_w, router_b, exp_w_gu, exp_b_gu, exp_w_down, exp_b_down):
    depth = ln_mix_g.shape[0]
    n_mixers = 2
    alpha = float((2 * depth) ** 0.25)
    bsz, seq, d = x_prompt.shape
    dec_b, dec_seq, _ = x_sample.shape
    n_prompt = bsz * seq
    n_sample = dec_b * dec_seq
    assert bsz == 1 and n_prompt % GMLP_CHUNK == 0 and GMLP_CHUNK % dec_seq == 0

    x = jnp.concatenate([x_prompt.reshape(n_prompt, d), x_sample.reshape(n_sample, d)], axis=0)
    xb = x.astype(BF16)
    v_s, ckv_p, kr_p, ckv_s, kr_s = [], [], [], [], []
    for i in range(depth):
        j = i // n_mixers
        if i % n_mixers == 0:
            u, v = _gmlp_in(xb, a_w_in[j])
            gated, vn = _gmlp_gate(u, v, a_ln_v_g[j], a_ln_v_b[j], a_w_s[j], a_b_s[j], n_prompt, dec_seq)
            pre = _matmul(gated, a_w_out[j], out_dtype=F32, res=x, alpha=alpha, name="gmlp_out")
            v_s.append(vn[n_prompt:].reshape(dec_b, dec_seq, -1))
        else:
            attn, ckv, kr = _mla_mixer(xb, n_prompt, dec_seq, cache_mla_ckv[j], cache_mla_krope[j],
                                       b_w_in[j], b_q_norm_g[j], b_kv_norm_g[j], b_w_uq[j], b_w_uk[j],
                                       b_w_uv[j])
            pre = _matmul(attn, b_w_out[j], out_dtype=F32, res=x, alpha=alpha, name="mla_out")
            ckv_p.append(ckv[:n_prompt].reshape(bsz, seq, -1))
            kr_p.append(kr[:n_prompt].reshape(bsz, seq, -1))
            ckv_s.append(ckv[n_prompt:].reshape(dec_b, dec_seq, -1))
            kr_s.append(kr[n_prompt:].reshape(dec_b, dec_seq, -1))
        x, xb = _moe_layer(pre, ln_mix_g[i], ln_mix_b[i], router_w[i], router_b[i], exp_w_gu, exp_b_gu,
                           exp_w_down, exp_b_down, i, ln_ffn_g[i], ln_ffn_b[i], alpha)

    return (x[:n_prompt].reshape(bsz, seq, d), x[n_prompt:].reshape(dec_b, dec_seq, d),
            jnp.stack(v_s), jnp.stack(ckv_p), jnp.stack(kr_p), jnp.stack(ckv_s), jnp.stack(kr_s))
```

```python
import functools

import numpy as np
import jax
import jax.numpy as jnp
from jax import lax
from jax.experimental import pallas as pl
from jax.experimental.pallas import tpu as pltpu

F32 = jnp.float32
BF16 = jnp.bfloat16

CHUNK = 64
GMLP_CHUNK = 128
GMLP_GROUPS = 8
TOP_K = 4
ROPE_THETA = 10000.0
NEG_INF = -1e30
SWIGLU_LIMIT = 7.0
SWIGLU_ALPHA = 1.702
LN_EPS = 1e-5
RMS_EPS = 1e-6

LANES = 128
MOE_SUPER_ROWS = 1536
MOE_SUB_ROWS = 128
MOE_DOWN_TILE = 1024
DMA_UNROLL = 8
VMEM_LIMIT = 56 * 1024 * 1024


def _tile(dim, pref):
    if dim <= pref:
        return dim
    t = pref
    while dim % t:
        t //= 2
    return t


def _params(*sem):
    return pltpu.CompilerParams(dimension_semantics=sem, vmem_limit_bytes=VMEM_LIMIT)


def _layer_norm(x, g, b):
    mu = jnp.mean(x, axis=-1, keepdims=True)
    xc = x - mu
    var = jnp.mean(xc * xc, axis=-1, keepdims=True)
    return xc * lax.rsqrt(var + LN_EPS) * g + b


def _rms_norm(x, g):
    return x * lax.rsqrt(jnp.mean(x * x, axis=-1, keepdims=True) + RMS_EPS) * g


def _gelu(x):
    return 0.5 * x * (1.0 + lax.erf(x * np.float32(1.0 / np.sqrt(2.0))))


def _swap_rope_halves(x):
    half = 32
    lane = lax.broadcasted_iota(jnp.int32, x.shape, 1)
    fwd = pltpu.roll(x, LANES - half, 1)
    bwd = pltpu.roll(x, half, 1)
    return jnp.where((lane % (2 * half)) < half, fwd, bwd)


def _mm_kernel(*refs, alpha, has_res):
    if has_res:
        x_ref, w_ref, r_ref, o_ref = refs
    else:
        x_ref, w_ref, o_ref = refs
    acc = jnp.dot(x_ref[...], w_ref[...].astype(BF16), preferred_element_type=F32)
    if has_res:
        acc = alpha * r_ref[...] + acc
    o_ref[...] = acc.astype(o_ref.dtype)


def _matmul(x, w, *, out_dtype, res=None, alpha=1.0, tm=1024, tn=512, name):
    m, k = x.shape
    n = w.shape[1]
    tm, tn = _tile(m, tm), _tile(n, tn)
    in_specs = [pl.BlockSpec((tm, k), lambda i, j: (i, 0)),
                pl.BlockSpec((k, tn), lambda i, j: (0, j))]
    args = [x, w]
    if res is not None:
        in_specs.append(pl.BlockSpec((tm, tn), lambda i, j: (i, j)))
        args.append(res)
    return pl.pallas_call(
        functools.partial(_mm_kernel, alpha=alpha, has_res=res is not None),
        grid=(m // tm, n // tn),
        in_specs=in_specs,
        out_specs=pl.BlockSpec((tm, tn), lambda i, j: (i, j)),
        out_shape=jax.ShapeDtypeStruct((m, n), out_dtype),
        compiler_params=_params("parallel", "arbitrary"),
        name=name,
    )(*args)


def _gmlp_in_kernel(x_ref, wu_ref, wv_ref, u_ref, v_ref):
    x = x_ref[...]
    u_ref[...] = _gelu(jnp.dot(x, wu_ref[...].astype(BF16), preferred_element_type=F32))
    v_ref[...] = _gelu(jnp.dot(x, wv_ref[...].astype(BF16), preferred_element_type=F32))


def _gmlp_in(xb, w_in):
    m, k = xb.shape
    width = w_in.shape[1] // 2
    tm, tn = _tile(m, 1024), _tile(width, 256)
    nj = width // tn
    return pl.pallas_call(
        _gmlp_in_kernel,
        grid=(m // tm, nj),
        in_specs=[pl.BlockSpec((tm, k), lambda i, j: (i, 0)),
                  pl.BlockSpec((k, tn), lambda i, j: (0, j)),
                  pl.BlockSpec((k, tn), lambda i, j: (0, j + nj))],
        out_specs=[pl.BlockSpec((tm, tn), lambda i, j: (i, j)),
                   pl.BlockSpec((tm, tn), lambda i, j: (i, j))],
        out_shape=[jax.ShapeDtypeStruct((m, width), F32)] * 2,
        compiler_params=_params("parallel", "arbitrary"),
        name="gmlp_in",
    )(xb, w_in, w_in)


def _gmlp_gate_kernel(u_ref, v_ref, g_ref, b_ref, ws_ref, bs_ref, o_ref, vn_ref, *, n_prompt_tiles,
                      sample_shift):
    i = pl.program_id(0)
    rows = u_ref.shape[0]
    gdim = u_ref.shape[1] // GMLP_GROUPS
    vn = _layer_norm(v_ref[...], g_ref[...], b_ref[...])
    vn_ref[...] = vn
    vnb = vn.astype(BF16)
    shift = jnp.where(i >= n_prompt_tiles, sample_shift, int(np.log2(GMLP_CHUNK)))
    r = lax.broadcasted_iota(jnp.int32, (rows, rows), 0)
    c = lax.broadcasted_iota(jnp.int32, (rows, rows), 1)
    keep = (c <= r) & ((r >> shift) == (c >> shift))
    bias = bs_ref[...]
    for g in range(GMLP_GROUPS):
        w = jnp.where(keep, ws_ref[g], 0.0).astype(BF16)
        s = jnp.dot(w, vnb[:, g * gdim:(g + 1) * gdim], preferred_element_type=F32)
        s = s + bias[:, g:g + 1]
        o_ref[:, g * gdim:(g + 1) * gdim] = (u_ref[:, g * gdim:(g + 1) * gdim] * s).astype(o_ref.dtype)


def _gmlp_gate(u, v, ln_g, ln_b, w_s, b_s, n_prompt, dec_seq):
    t, width = u.shape
    rows = GMLP_CHUNK
    reps = rows // dec_seq
    w_prompt = w_s
    w_sample = jnp.tile(w_s[:, :dec_seq, :dec_seq], (1, reps, reps))
    b_prompt = b_s.T
    b_sample = jnp.tile(b_s[:, :dec_seq].T, (reps, 1))
    w_all = jnp.stack([w_prompt, w_sample])
    b_all = jnp.stack([b_prompt, b_sample])
    n_prompt_tiles = n_prompt // rows
    sel = lambda i: (i >= n_prompt_tiles).astype(jnp.int32)
    return pl.pallas_call(
        functools.partial(_gmlp_gate_kernel, n_prompt_tiles=n_prompt_tiles,
                          sample_shift=int(np.log2(dec_seq))),
        grid=(t // rows,),
        in_specs=[pl.BlockSpec((rows, width), lambda i: (i, 0)),
                  pl.BlockSpec((rows, width), lambda i: (i, 0)),
                  pl.BlockSpec((1, width), lambda i: (0, 0)),
                  pl.BlockSpec((1, width), lambda i: (0, 0)),
                  pl.BlockSpec((None, GMLP_GROUPS, rows, rows), lambda i: (sel(i), 0, 0, 0)),
                  pl.BlockSpec((None, rows, GMLP_GROUPS), lambda i: (sel(i), 0, 0))],
        out_specs=[pl.BlockSpec((rows, width), lambda i: (i, 0)),
                   pl.BlockSpec((rows, width), lambda i: (i, 0))],
        out_shape=[jax.ShapeDtypeStruct((t, width), BF16), jax.ShapeDtypeStruct((t, width), F32)],
        compiler_params=_params("parallel"),
        name="gmlp_gate",
    )(u, v, ln_g.reshape(1, -1), ln_b.reshape(1, -1), w_all, b_all)


def _ln_router_kernel(p_ref, g_ref, b_ref, rw_ref, rb_ref, y_ref, pk_ref, idx_ref, rank_ref, gate_ref,
                      count_ref, cnt_ref):
    y = _layer_norm(p_ref[...], g_ref[...], b_ref[...])
    y_ref[...] = y
    rw = rw_ref[...]
    y_hi, w_hi = y.astype(BF16), rw.astype(BF16)
    half = y.shape[1] // 2
    y_bits = lax.bitcast_convert_type(y_hi.astype(F32), jnp.uint32)
    pk_ref[...] = (y_bits[:, :half] >> 16) | (y_bits[:, half:] & jnp.uint32(0xFFFF0000))
    y_lo = (y - y_hi.astype(F32)).astype(BF16)
    w_lo = (rw - w_hi.astype(F32)).astype(BF16)
    logits = (jnp.dot(y_hi, w_hi, preferred_element_type=F32)
              + (jnp.dot(y_lo, w_hi, preferred_element_type=F32)
                 + jnp.dot(y_hi, w_lo, preferred_element_type=F32))) + rb_ref[...]
    n_e = logits.shape[1]
    col = lax.broadcasted_iota(jnp.int32, logits.shape, 1)
    vals, idxs = [], []
    cur = logits
    for _ in range(TOP_K):
        mx = jnp.max(cur, axis=-1, keepdims=True)
        am = jnp.min(jnp.where(cur == mx, col, n_e), axis=-1, keepdims=True)
        vals.append(mx)
        idxs.append(am)
        cur = jnp.where(col == am, -jnp.inf, cur)
    es = [jnp.exp(v - vals[0]) for v in vals]
    denom = es[0]
    for e in es[1:]:
        denom = denom + e
    tm = logits.shape[0]

    @pl.when(pl.program_id(0) == 0)
    def _():
        cnt_ref[...] = jnp.zeros_like(cnt_ref)
    onehots = [(col == am).astype(F32) for am in idxs]
    tile_hot = onehots[0]
    for oh in onehots[1:]:
        tile_hot = tile_hot + oh
    lower = (lax.broadcasted_iota(jnp.int32, (tm, tm), 1)
             < lax.broadcasted_iota(jnp.int32, (tm, tm), 0)).astype(BF16)
    before = jnp.dot(lower, tile_hot.astype(BF16), preferred_element_type=F32) + cnt_ref[...]
    cnt_ref[...] = cnt_ref[...] + jnp.sum(tile_hot, axis=0, keepdims=True)
    count_ref[...] = cnt_ref[...]

    kcol = lax.broadcasted_iota(jnp.int32, (tm, TOP_K), 1)
    idx_out = jnp.zeros((tm, TOP_K), jnp.int32)
    rank_out = jnp.zeros((tm, TOP_K), jnp.int32)
    gate_out = jnp.zeros((tm, TOP_K), F32)
    for k in range(TOP_K):
        rank_k = jnp.sum(onehots[k] * before, axis=-1, keepdims=True).astype(jnp.int32)
        idx_out = jnp.where(kcol == k, idxs[k], idx_out)
        rank_out = jnp.where(kcol == k, rank_k, rank_out)
        gate_out = jnp.where(kcol == k, es[k] / denom, gate_out)
    idx_ref[...] = idx_out
    rank_ref[...] = rank_out
    gate_ref[...] = gate_out


def _ln_router(pre, g, b, rw, rb):
    t, d = pre.shape
    n_e = rw.shape[1]
    tm = _tile(t, 256)
    return pl.pallas_call(
        _ln_router_kernel,
        grid=(t // tm,),
        in_specs=[pl.BlockSpec((tm, d), lambda i: (i, 0)),
                  pl.BlockSpec((1, d), lambda i: (0, 0)),
                  pl.BlockSpec((1, d), lambda i: (0, 0)),
                  pl.BlockSpec((d, n_e), lambda i: (0, 0)),
                  pl.BlockSpec((1, n_e), lambda i: (0, 0))],
        out_specs=[pl.BlockSpec((tm, d), lambda i: (i, 0)),
                   pl.BlockSpec((tm, d // 2), lambda i: (i, 0)),
                   pl.BlockSpec((tm, TOP_K), lambda i: (i, 0)),
                   pl.BlockSpec((tm, TOP_K), lambda i: (i, 0)),
                   pl.BlockSpec((tm, TOP_K), lambda i: (i, 0)),
                   pl.BlockSpec((1, n_e), lambda i: (0, 0))],
        out_shape=[jax.ShapeDtypeStruct((t, d), F32),
                   jax.ShapeDtypeStruct((t, d // 2), jnp.uint32),
                   jax.ShapeDtypeStruct((t, TOP_K), jnp.int32),
                   jax.ShapeDtypeStruct((t, TOP_K), jnp.int32),
                   jax.ShapeDtypeStruct((t, TOP_K), F32),
                   jax.ShapeDtypeStruct((1, n_e), F32)],
        scratch_shapes=[pltpu.VMEM((1, n_e), F32)],
        compiler_params=_params("arbitrary"),
        name="ln_router",
    )(pre, g.reshape(1, -1), b.reshape(1, -1), rw, rb.reshape(1, -1))


def _route_tables(top_idx, rank, counts, n_e, n_sb):
    t = top_idx.shape[0]
    flat_e = top_idx.reshape(-1)
    rank = rank.reshape(-1)
    counts = counts.reshape(-1).astype(jnp.int32)
    sb_per_e = (counts + MOE_SUPER_ROWS - 1) // MOE_SUPER_ROWS
    per_sb = -(-counts // jnp.maximum(sb_per_e, 1))
    per_sb = jnp.maximum((per_sb + MOE_SUB_ROWS - 1) // MOE_SUB_ROWS * MOE_SUB_ROWS, MOE_SUB_ROWS)
    sb_end = jnp.cumsum(sb_per_e)
    sb_first = sb_end - sb_per_e
    n_used = sb_end[-1]
    sb = jnp.arange(n_sb, dtype=jnp.int32)
    sb_c = jnp.minimum(sb, n_used - 1)
    e_of = jnp.minimum(jnp.searchsorted(sb_end, sb_c, side="right"), n_e - 1).astype(jnp.int32)
    rows = jnp.clip(counts[e_of] - (sb_c - sb_first[e_of]) * per_sb[e_of], 0, per_sb[e_of])
    rows = jnp.where(sb < n_used, rows, 0).astype(jnp.int32)
    a_sb = sb_first[flat_e] + rank // per_sb[flat_e]
    dest = (a_sb * MOE_SUPER_ROWS + rank % per_sb[flat_e]).astype(jnp.int32)
    tok = (jnp.arange(t * TOP_K, dtype=jnp.int32) // TOP_K)
    slot_tok = jnp.zeros((n_sb * MOE_SUPER_ROWS,), jnp.int32).at[dest].set(tok)
    return e_of, rows, slot_tok, dest


def _row_blocks(rows, compute, zero):
    sub = MOE_SUB_ROWS
    nb = (rows + sub - 1) // sub
    n_big = nb // 4
    rem = nb % 4

    def big(s, c):
        compute(pl.multiple_of(s * 4 * sub, 4 * sub), 4 * sub)
        return c
    lax.fori_loop(0, n_big, big, 0)

    @pl.when(rem >= 2)
    def _():
        compute(pl.multiple_of(n_big * 4 * sub, 2 * sub), 2 * sub)

    @pl.when(rem % 2 == 1)
    def _():
        compute(pl.multiple_of((nb - 1) * sub, sub), sub)

    def zero_body(s, c):
        zero(pl.multiple_of(s * sub, sub))
        return c
    lax.fori_loop(nb, MOE_SUPER_ROWS // sub, zero_body, 0)


def _moe_up_kernel(sb_e, sb_rows, slot_tok, y_hbm, wg_ref, wu_ref, bg_ref, bu_ref, act_ref,
                   xpk, xb16, wb16, sem, *, n_sb, nj):
    sb = pl.program_id(0)
    j = pl.program_id(1)
    tn = wg_ref.shape[1]
    sub = MOE_SUB_ROWS
    per_sub = sub // nj
    rows = sb_rows[sb]

    def padded(r, m):
        return (r + m - 1) // m * m

    p_cur = padded(rows, sub)
    base_next = jnp.minimum(sb + 1, n_sb - 1) * MOE_SUPER_ROWS

    def row_copy(tok, r):
        return pltpu.make_async_copy(y_hbm.at[pl.ds(tok, 1)], xpk.at[pl.ds(r, 1)], sem.at[0])

    def wait_rows(n):
        n = pl.multiple_of(n, DMA_UNROLL)
        pltpu.make_async_copy(y_hbm.at[pl.ds(0, n)], xpk.at[pl.ds(0, n)], sem.at[0]).wait()

    def fetch_loop(sb_i, lo, hi):
        base = sb_i * MOE_SUPER_ROWS

        def body_grp(g, c):
            r0 = g * DMA_UNROLL
            for u in range(DMA_UNROLL):
                row_copy(slot_tok[base + r0 + u], r0 + u).start()
            return c
        lax.fori_loop(lo // DMA_UNROLL, hi // DMA_UNROLL, body_grp, 0)

    @pl.when((sb == 0) & (j == 0))
    def _():
        xpk[...] = jnp.zeros_like(xpk)
        fetch_loop(0, 0, padded(sb_rows[0], DMA_UNROLL))

    @pl.when(j == 0)
    def _():
        p_prev = jnp.where(sb > 0, padded(sb_rows[jnp.maximum(sb - 1, 0)], sub), 0)
        n_wait = jnp.maximum(p_prev, padded(rows, DMA_UNROLL))

        @pl.when(n_wait > 0)
        def _():
            wait_rows(n_wait)
        half = xpk.shape[1]
        pk = xpk[...]
        xb16[:, :half] = lax.bitcast_convert_type(pk << 16, F32).astype(BF16)
        xb16[:, half:] = lax.bitcast_convert_type(pk & jnp.uint32(0xFFFF0000), F32).astype(BF16)

        @pl.when(sb + 1 < n_sb)
        def _():
            fetch_loop(sb + 1, p_cur, padded(sb_rows[sb + 1], DMA_UNROLL))

    @pl.when(rows > 0)
    def _():
        wb16[:, :tn] = wg_ref[...].astype(BF16)
        wb16[:, tn:] = wu_ref[...].astype(BF16)

    def compute(r0, n):
        k0 = j * (p_cur // nj) + (r0 // sub) * per_sub
        for u in range((n // sub) * per_sub):
            row_copy(slot_tok[base_next + k0 + u], k0 + u).start()
        h = jnp.dot(xb16[pl.ds(r0, n), :], wb16[...], preferred_element_type=F32)
        gate = jnp.minimum(h[:, :tn] + bg_ref[...], SWIGLU_LIMIT)
        up = jnp.clip(h[:, tn:] + bu_ref[...], -SWIGLU_LIMIT, SWIGLU_LIMIT)
        a = (up + 1.0) * gate * jax.nn.sigmoid(SWIGLU_ALPHA * gate)
        act_ref[pl.ds(r0, n), :] = a.astype(act_ref.dtype)

    def zero(r0):
        act_ref[pl.ds(r0, MOE_SUB_ROWS), :] = jnp.zeros((MOE_SUB_ROWS, tn), act_ref.dtype)

    _row_blocks(rows, compute, zero)

    @pl.when((sb == n_sb - 1) & (j == nj - 1) & (p_cur > 0))
    def _():
        wait_rows(p_cur)


def _moe_up(y_pk, w_gu, b_gu, layer, sb_e, sb_rows, slot_tok):
    _, n_e, d, ff2 = w_gu.shape
    assert y_pk.shape[1] * 2 == d
    ff = ff2 // 2
    n_sb = sb_e.shape[0]
    tn = _tile(ff, 256)
    nj = ff // tn
    assert MOE_SUB_ROWS % nj == 0 and MOE_SUPER_ROWS % (4 * MOE_SUB_ROWS) == 0
    b4 = b_gu.reshape(b_gu.shape[0], n_e, 1, ff2)

    def jw(sb, j, rows):
        return jnp.where(rows[sb] > 0, j, nj - 1)

    grid_spec = pltpu.PrefetchScalarGridSpec(
        num_scalar_prefetch=3,
        grid=(n_sb, nj),
        in_specs=[pl.BlockSpec(memory_space=pl.ANY),
                  pl.BlockSpec((None, None, d, tn), lambda sb, j, e, rows, st: (layer, e[sb], 0, jw(sb, j, rows))),
                  pl.BlockSpec((None, None, d, tn), lambda sb, j, e, rows, st: (layer, e[sb], 0, jw(sb, j, rows) + nj)),
                  pl.BlockSpec((None, None, 1, tn), lambda sb, j, e, rows, st: (layer, e[sb], 0, jw(sb, j, rows))),
                  pl.BlockSpec((None, None, 1, tn), lambda sb, j, e, rows, st: (layer, e[sb], 0, jw(sb, j, rows) + nj))],
        out_specs=pl.BlockSpec((MOE_SUPER_ROWS, tn), lambda sb, j, e, rows, st: (sb, j)),
        scratch_shapes=[pltpu.VMEM((MOE_SUPER_ROWS, d // 2), jnp.uint32),
                        pltpu.VMEM((MOE_SUPER_ROWS, d), BF16),
                        pltpu.VMEM((d, 2 * tn), BF16),
                        pltpu.SemaphoreType.DMA((1,))],
    )
    return pl.pallas_call(
        functools.partial(_moe_up_kernel, n_sb=n_sb, nj=nj),
        grid_spec=grid_spec,
        out_shape=jax.ShapeDtypeStruct((n_sb * MOE_SUPER_ROWS, ff), BF16),
        compiler_params=_params("arbitrary", "arbitrary"),
        name="moe_up",
    )(sb_e, sb_rows, slot_tok, y_pk, w_gu, w_gu, b4, b4)


def _moe_down_kernel(sb_e, sb_rows, a_ref, w_ref, b_ref, o_ref, wb16):
    sb = pl.program_id(0)
    tn = w_ref.shape[1]
    rows = sb_rows[sb]

    @pl.when(rows > 0)
    def _():
        wb16[...] = w_ref[...].astype(BF16)

    def compute(r0, n):
        o = jnp.dot(a_ref[pl.ds(r0, n), :], wb16[...], preferred_element_type=F32) + b_ref[...]
        bits = lax.bitcast_convert_type(o.astype(BF16).astype(F32), jnp.uint32)
        o_ref[pl.ds(r0, n), :] = (bits[:, :tn // 2] >> 16) | (bits[:, tn // 2:] & jnp.uint32(0xFFFF0000))

    def zero(r0):
        o_ref[pl.ds(r0, MOE_SUB_ROWS), :] = jnp.zeros((MOE_SUB_ROWS, tn // 2), jnp.uint32)

    _row_blocks(rows, compute, zero)


def _moe_down(act, w_down, b_down, layer, sb_e, sb_rows):
    n_slots, ff = act.shape
    _, n_e, _, d = w_down.shape
    n_sb = sb_e.shape[0]
    tn = _tile(d, MOE_DOWN_TILE)
    nj = d // tn
    b4 = b_down.reshape(b_down.shape[0], n_e, 1, d)

    def jw(sb, j, rows):
        return jnp.where(rows[sb] > 0, j, nj - 1)

    grid_spec = pltpu.PrefetchScalarGridSpec(
        num_scalar_prefetch=2,
        grid=(n_sb, nj),
        in_specs=[pl.BlockSpec((MOE_SUPER_ROWS, ff), lambda sb, j, e, rows: (sb, 0)),
                  pl.BlockSpec((None, None, ff, tn), lambda sb, j, e, rows: (layer, e[sb], 0, jw(sb, j, rows))),
                  pl.BlockSpec((None, None, 1, tn), lambda sb, j, e, rows: (layer, e[sb], 0, jw(sb, j, rows)))],
        out_specs=pl.BlockSpec((MOE_SUPER_ROWS, tn // 2), lambda sb, j, e, rows: (sb, j)),
        scratch_shapes=[pltpu.VMEM((ff, tn), BF16)],
    )
    return pl.pallas_call(
        _moe_down_kernel,
        grid_spec=grid_spec,
        out_shape=jax.ShapeDtypeStruct((n_slots, d // 2), jnp.uint32),
        compiler_params=_params("arbitrary", "arbitrary"),
        name="moe_down",
    )(sb_e, sb_rows, act, w_down, b4)


def _moe_combine_kernel(dest, ys_hbm, gate_ref, y_ref, g_ref, b_ref, o_ref, ob_ref, gbuf, sem, *, alpha):
    i = pl.program_id(0)
    n = pl.num_programs(0)
    tm = y_ref.shape[0]
    slot = i % 2

    def issue(tile, s):
        base = tile * tm * TOP_K

        def body(r, c):
            for k in range(TOP_K):
                pltpu.make_async_copy(ys_hbm.at[pl.ds(dest[base + r * TOP_K + k], 1)],
                                      gbuf.at[s, k, pl.ds(r, 1)], sem.at[s]).start()
            return c
        lax.fori_loop(0, tm, body, 0, unroll=DMA_UNROLL // TOP_K)

    @pl.when(i == 0)
    def _():
        issue(0, 0)

    @pl.when(i + 1 < n)
    def _():
        issue(i + 1, 1 - slot)

    for k in range(TOP_K):
        pltpu.make_async_copy(ys_hbm.at[pl.ds(0, tm)], gbuf.at[slot, k], sem.at[slot]).wait()

    gates = gate_ref[...]
    d = y_ref.shape[1]
    tile = min(MOE_DOWN_TILE, d)
    pieces = []
    for jt in range(d // tile):
        lo_sum = hi_sum = None
        for k in range(TOP_K):
            w = gbuf[slot, k, :, jt * (tile // 2):(jt + 1) * (tile // 2)]
            lo = gates[:, k:k + 1] * lax.bitcast_convert_type(w << 16, F32)
            hi = gates[:, k:k + 1] * lax.bitcast_convert_type(w & jnp.uint32(0xFFFF0000), F32)
            lo_sum = lo if k == 0 else lo_sum + lo
            hi_sum = hi if k == 0 else hi_sum + hi
        pieces += [lo_sum, hi_sum]
    moe = jnp.concatenate(pieces, axis=1)
    out = _layer_norm(alpha * y_ref[...] + moe, g_ref[...], b_ref[...])
    o_ref[...] = out
    ob_ref[...] = out.astype(ob_ref.dtype)


def _moe_combine(ys, dest, gates, y, g, b, alpha):
    t, d = y.shape
    tm = _tile(t, 128)
    grid_spec = pltpu.PrefetchScalarGridSpec(
        num_scalar_prefetch=1,
        grid=(t // tm,),
        in_specs=[pl.BlockSpec(memory_space=pl.ANY),
                  pl.BlockSpec((tm, TOP_K), lambda i, dst: (i, 0)),
                  pl.BlockSpec((tm, d), lambda i, dst: (i, 0)),
                  pl.BlockSpec((1, d), lambda i, dst: (0, 0)),
                  pl.BlockSpec((1, d), lambda i, dst: (0, 0))],
        out_specs=[pl.BlockSpec((tm, d), lambda i, dst: (i, 0)),
                   pl.BlockSpec((tm, d), lambda i, dst: (i, 0))],
        scratch_shapes=[pltpu.VMEM((2, TOP_K, tm, d // 2), jnp.uint32),
                        pltpu.SemaphoreType.DMA((2,))],
    )
    return pl.pallas_call(
        functools.partial(_moe_combine_kernel, alpha=alpha),
        grid_spec=grid_spec,
        out_shape=[jax.ShapeDtypeStruct((t, d), F32), jax.ShapeDtypeStruct((t, d), BF16)],
        compiler_params=_params("arbitrary"),
        name="moe_combine",
    )(dest, ys, gates, y, g.reshape(1, -1), b.reshape(1, -1))


def _moe_layer(pre, ln_g, ln_b, rw, rb, w_gu, b_gu, w_down, b_down, layer, ffn_g, ffn_b, alpha):
    t = pre.shape[0]
    n_e = rw.shape[1]
    y, y_pk, top_idx, rank, gates, counts = _ln_router(pre, ln_g, ln_b, rw, rb)
    n_sb = (t * TOP_K) // MOE_SUPER_ROWS + n_e
    sb_e, sb_rows, slot_tok, dest = _route_tables(top_idx, rank, counts, n_e, n_sb)
    act = _moe_up(y_pk, w_gu, b_gu, layer, sb_e, sb_rows, slot_tok)
    ys = _moe_down(act, w_down, b_down, layer, sb_e, sb_rows)
    return _moe_combine(ys, dest, gates, y, ffn_g, ffn_b, alpha)


def _mla_latent_kernel(h_ref, qg_ref, kvg_ref, cos_ref, sin_ref, cq_ref, ckv_ref, ckvb_ref, kr_ref,
                       krp_ref, *, q_lora, kv_lora, rope_dim):
    h = h_ref[...]
    cq_ref[...] = _rms_norm(h[:, :q_lora], qg_ref[...]).astype(cq_ref.dtype)
    ckv = _rms_norm(h[:, q_lora:q_lora + kv_lora], kvg_ref[...])
    ckv_ref[...] = ckv
    ckvb_ref[...] = ckv.astype(ckvb_ref.dtype)
    x = h[:, q_lora + kv_lora:q_lora + kv_lora + LANES]
    kr = x * cos_ref[...] + _swap_rope_halves(x) * sin_ref[...]
    kr_ref[...] = kr[:, :rope_dim]
    krp_ref[...] = kr.astype(krp_ref.dtype)


def _mla_latent(h, q_g, kv_g, cos_t, sin_t, q_lora, kv_lora, rope_dim):
    t, hw = h.shape
    tm = _tile(t, 512)
    return pl.pallas_call(
        functools.partial(_mla_latent_kernel, q_lora=q_lora, kv_lora=kv_lora, rope_dim=rope_dim),
        grid=(t // tm,),
        in_specs=[pl.BlockSpec((tm, hw), lambda i: (i, 0)),
                  pl.BlockSpec((1, q_lora), lambda i: (0, 0)),
                  pl.BlockSpec((1, kv_lora), lambda i: (0, 0)),
                  pl.BlockSpec((tm, LANES), lambda i: (i, 0)),
                  pl.BlockSpec((tm, LANES), lambda i: (i, 0))],
        out_specs=[pl.BlockSpec((tm, q_lora), lambda i: (i, 0)),
                   pl.BlockSpec((tm, kv_lora), lambda i: (i, 0)),
                   pl.BlockSpec((tm, kv_lora), lambda i: (i, 0)),
                   pl.BlockSpec((tm, rope_dim), lambda i: (i, 0)),
                   pl.BlockSpec((tm, LANES), lambda i: (i, 0))],
        out_shape=[jax.ShapeDtypeStruct((t, q_lora), BF16),
                   jax.ShapeDtypeStruct((t, kv_lora), F32),
                   jax.ShapeDtypeStruct((t, kv_lora), BF16),
                   jax.ShapeDtypeStruct((t, rope_dim), F32),
                   jax.ShapeDtypeStruct((t, LANES), BF16)],
        compiler_params=_params("parallel"),
        name="mla_latent",
    )(h, q_g.reshape(1, -1), kv_g.reshape(1, -1), cos_t, sin_t)


def _q_proj_kernel(x_ref, w_ref, cos_ref, sin_ref, o_ref, *, scale):
    acc = jnp.dot(x_ref[...], w_ref[...].astype(BF16), preferred_element_type=F32) * scale
    cos, sin = cos_ref[...], sin_ref[...]
    for g in range(acc.shape[1] // LANES):
        blk = acc[:, g * LANES:(g + 1) * LANES]
        if g % 2:
            blk = blk * cos + _swap_rope_halves(blk) * sin
        o_ref[:, g * LANES:(g + 1) * LANES] = blk.astype(o_ref.dtype)


def _q_proj(cq, w_q, cos_t, sin_t, scale):
    t, k = cq.shape
    n = w_q.shape[1]
    tm, tn = _tile(t, 1024), _tile(n, 512)
    return pl.pallas_call(
        functools.partial(_q_proj_kernel, scale=scale),
        grid=(t // tm, n // tn),
        in_specs=[pl.BlockSpec((tm, k), lambda i, j: (i, 0)),
                  pl.BlockSpec((k, tn), lambda i, j: (0, j)),
                  pl.BlockSpec((tm, LANES), lambda i, j: (i, 0)),
                  pl.BlockSpec((tm, LANES), lambda i, j: (i, 0))],
        out_specs=pl.BlockSpec((tm, tn), lambda i, j: (i, j)),
        out_shape=jax.ShapeDtypeStruct((t, n), BF16),
        compiler_params=_params("parallel", "arbitrary"),
        name="q_proj",
    )(cq, w_q, cos_t, sin_t)


def _prompt_attn_kernel(q_ref, kn_ref, kr_ref, v_ref, o_ref, kcat, vcat, s_e0, s_e1, s_o0, s_o1,
                        m_0, m_1, acc_0, acc_1, *, tq, tk):
    qi = pl.program_id(1)
    hd = 2 * LANES
    s_even, s_odd = (s_e0, s_e1), (s_o0, s_o1)
    m_refs, acc_refs = (m_0, m_1), (acc_0, acc_1)

    @pl.when(qi == 0)
    def _():
        for a in range(2):
            kcat[a, :, :LANES] = kn_ref[:, a * LANES:(a + 1) * LANES]
            kcat[a, :, LANES:] = kr_ref[...]
            vcat[a, :, :LANES] = v_ref[:, a * LANES:(a + 1) * LANES]
            vcat[a, :, LANES:] = jnp.ones((vcat.shape[1], LANES), vcat.dtype)

    for a in range(2):
        m_refs[a][...] = jnp.full(m_refs[a].shape, NEG_INF, F32)
        acc_refs[a][...] = jnp.zeros(acc_refs[a].shape, F32)

    def scores(kb, bufs, r_lo=0):
        k0 = pl.multiple_of(kb * tk, tk)
        for a in range(2):
            bufs[a][r_lo:, :] = lax.dot_general(q_ref[r_lo:, a * hd:(a + 1) * hd], kcat[a, pl.ds(k0, tk), :],
                                                (((1,), (1,)), ((), ())), preferred_element_type=F32)

    def softmax_pv(kb, bufs, masked, r_lo=0):
        k0 = pl.multiple_of(kb * tk, tk)
        if masked:
            r = qi * tq + r_lo + lax.broadcasted_iota(jnp.int32, (tq - r_lo, tk), 0)
            c = k0 + lax.broadcasted_iota(jnp.int32, (tq - r_lo, tk), 1)
            ok = (c // CHUNK) <= (r // CHUNK)
        for a in range(2):
            s = bufs[a][r_lo:, :]
            if masked:
                s = jnp.where(ok, s, NEG_INF)
            m_prev = m_refs[a][r_lo:, :]
            m_new = jnp.maximum(m_prev, jnp.max(s, axis=-1, keepdims=True))
            p = jnp.exp(s - m_new).astype(BF16)
            acc_refs[a][r_lo:, :] = jnp.exp(m_prev - m_new) * acc_refs[a][r_lo:, :] + jnp.dot(
                p, vcat[a, pl.ds(k0, tk), :], preferred_element_type=F32)
            m_refs[a][r_lo:, :] = m_new

    n_pair = (qi * tq) // (2 * tk)
    scores(0, s_even)

    def body(i, carry):
        kb = 2 * i
        scores(kb + 1, s_odd)
        softmax_pv(kb, s_even, False)
        scores(kb + 2, s_even)
        softmax_pv(kb + 1, s_odd, False)
        return carry
    lax.fori_loop(0, n_pair, body, 0)
    kb = 2 * n_pair
    scores(kb + 1, s_odd, tk)
    softmax_pv(kb, s_even, True)
    softmax_pv(kb + 1, s_odd, True, tk)
    for a in range(2):
        o_ref[:, a * LANES:(a + 1) * LANES] = (
            acc_refs[a][:, :LANES] / acc_refs[a][:, LANES:]).astype(o_ref.dtype)


def _prompt_attn(q, kn, krp, v, n_prompt, n_heads):
    tq = _tile(n_prompt, 1024)
    tk = tq // 2
    assert tk % CHUNK == 0 and n_prompt % tq == 0
    return pl.pallas_call(
        functools.partial(_prompt_attn_kernel, tq=tq, tk=tk),
        grid=(n_heads // 2, n_prompt // tq),
        in_specs=[pl.BlockSpec((tq, 4 * LANES), lambda p, i: (i, p)),
                  pl.BlockSpec((n_prompt, 2 * LANES), lambda p, i: (0, p), pipeline_mode=pl.Buffered(1)),
                  pl.BlockSpec((n_prompt, LANES), lambda p, i: (0, 0), pipeline_mode=pl.Buffered(1)),
                  pl.BlockSpec((n_prompt, 2 * LANES), lambda p, i: (0, p), pipeline_mode=pl.Buffered(1))],
        out_specs=pl.BlockSpec((tq, 2 * LANES), lambda p, i: (i, p)),
        out_shape=jax.ShapeDtypeStruct((n_prompt, n_heads * LANES), BF16),
        scratch_shapes=([pltpu.VMEM((2, n_prompt, 2 * LANES), BF16)] * 2
                        + [pltpu.VMEM((tq, tk), F32)] * 4
                        + [pltpu.VMEM((tq, 1), F32)] * 2
                        + [pltpu.VMEM((tq, 2 * LANES), F32)] * 2),
        compiler_params=_params("arbitrary", "arbitrary"),
        name="prompt_attn",
    )(q, kn, krp, v)


def _q_latent_kernel(q_ref, wuk_ref, o_ref, *, kv_lora):
    qn = q_ref[:, :LANES]
    q_lat = lax.dot_general(qn, wuk_ref[...].astype(BF16), (((1,), (1,)), ((), ())),
                            preferred_element_type=F32)
    o_ref[:, :kv_lora] = q_lat.astype(o_ref.dtype)
    o_ref[:, kv_lora:] = q_ref[:, LANES:]


def _q_latent(q, w_uk_h, n_prompt):
    n_heads, kv_lora, _ = w_uk_h.shape
    n_s = q.shape[0] - n_prompt
    assert n_prompt % n_s == 0
    rb = n_prompt // n_s
    return pl.pallas_call(
        functools.partial(_q_latent_kernel, kv_lora=kv_lora),
        grid=(n_heads,),
        in_specs=[pl.BlockSpec((n_s, 2 * LANES), lambda h: (rb, h)),
                  pl.BlockSpec((None, kv_lora, LANES), lambda h: (h, 0, 0))],
        out_specs=pl.BlockSpec((None, n_s, kv_lora + LANES), lambda h: (h, 0, 0)),
        out_shape=jax.ShapeDtypeStruct((n_heads, n_s, kv_lora + LANES), BF16),
        compiler_params=_params("parallel"),
        name="q_latent",
    )(q, w_uk_h)


def _sample_attn_kernel(q_ref, cc_ref, ck_ref, nc_ref, nk_ref, o_ref, kcat, *, past, dec_seq, kv_lora,
                        n_keys_pad):
    n_heads = q_ref.shape[0]
    n_q = n_heads * dec_seq
    kw = kv_lora + LANES
    kcat[0:past, 0:kv_lora] = cc_ref[...].astype(BF16)
    kcat[0:past, kv_lora:kw] = ck_ref[...].astype(BF16)
    kcat[past:past + dec_seq, 0:kv_lora] = nc_ref[...]
    kcat[past:past + dec_seq, kv_lora:kw] = nk_ref[...]
    kcat[past + dec_seq:n_keys_pad, :] = jnp.zeros((n_keys_pad - past - dec_seq, kw), BF16)
    q = q_ref[...].reshape(n_q, kw)
    s = lax.dot_general(q, kcat[...], (((1,), (1,)), ((), ())), preferred_element_type=F32)
    q_pos = past + lax.broadcasted_iota(jnp.int32, s.shape, 0) % dec_seq
    k_pos = lax.broadcasted_iota(jnp.int32, s.shape, 1)
    ok = ((k_pos // CHUNK) <= (q_pos // CHUNK)) & (k_pos < past + dec_seq)
    s = jnp.where(ok, s, NEG_INF)
    p = jnp.exp(s - jnp.max(s, axis=-1, keepdims=True))
    p = p / jnp.sum(p, axis=-1, keepdims=True)
    o = jnp.dot(p.astype(BF16), kcat[:, 0:kv_lora], preferred_element_type=F32)
    o_ref[...] = o.astype(o_ref.dtype).reshape(n_heads, dec_seq, kv_lora)


def _sample_attn(q_lat, cache_ckv, cache_krp, ckv_b, krp_b, row0, dec_seq):
    n_heads, n_s, kw = q_lat.shape
    n_b, past, kv_lora = cache_ckv.shape
    n_keys_pad = -(-(past + dec_seq) // LANES) * LANES
    rb0 = row0 // dec_seq
    return pl.pallas_call(
        functools.partial(_sample_attn_kernel, past=past, dec_seq=dec_seq, kv_lora=kv_lora,
                          n_keys_pad=n_keys_pad),
        grid=(n_b,),
        in_specs=[pl.BlockSpec((n_heads, dec_seq, kw), lambda b: (0, b, 0)),
                  pl.BlockSpec((None, past, kv_lora), lambda b: (b, 0, 0)),
                  pl.BlockSpec((None, past, LANES), lambda b: (b, 0, 0)),
                  pl.BlockSpec((dec_seq, kv_lora), lambda b: (rb0 + b, 0)),
                  pl.BlockSpec((dec_seq, LANES), lambda b: (rb0 + b, 0))],
        out_specs=pl.BlockSpec((n_heads, dec_seq, kv_lora), lambda b: (0, b, 0)),
        out_shape=jax.ShapeDtypeStruct((n_heads, n_s, kv_lora), BF16),
        scratch_shapes=[pltpu.VMEM((n_keys_pad, kw), BF16)],
        compiler_params=_params("parallel"),
        name="sample_attn",
    )(q_lat, cache_ckv, cache_krp, ckv_b, krp_b)


def _o_latent_kernel(o_ref, wuv_ref, out_ref):
    out_ref[...] = jnp.dot(o_ref[...], wuv_ref[...].astype(BF16),
                           preferred_element_type=F32).astype(out_ref.dtype)


def _o_latent(o_lat, w_uv_h):
    n_heads, n_s, kv_lora = o_lat.shape
    return pl.pallas_call(
        _o_latent_kernel,
        grid=(n_heads,),
        in_specs=[pl.BlockSpec((None, n_s, kv_lora), lambda h: (h, 0, 0)),
                  pl.BlockSpec((None, kv_lora, LANES), lambda h: (h, 0, 0))],
        out_specs=pl.BlockSpec((n_s, LANES), lambda h: (0, h)),
        out_shape=jax.ShapeDtypeStruct((n_s, n_heads * LANES), BF16),
        compiler_params=_params("parallel"),
        name="o_latent",
    )(o_lat, w_uv_h)


def _rope_tables(pos, rope_dim):
    half = rope_dim // 2
    inv_freq = ROPE_THETA ** (-jnp.arange(half, dtype=F32) / half)
    ang = pos.astype(F32)[:, None] * inv_freq[None, :]
    cos, sin = jnp.cos(ang), jnp.sin(ang)
    zeros = jnp.zeros((pos.shape[0], LANES - rope_dim), F32)
    return (jnp.concatenate([cos, cos, zeros], axis=1), jnp.concatenate([-sin, sin, zeros], axis=1))


def _mla_mixer(xb, n_prompt, dec_seq, cache_ckv, cache_krope, w_in, q_g, kv_g, w_uq, w_uk, w_uv):
    t = xb.shape[0]
    q_lora = q_g.shape[0]
    kv_lora, n_heads, nope = w_uk.shape
    rope_dim = w_uq.shape[2] - nope
    v_dim = w_uv.shape[2]
    assert nope == LANES and v_dim == LANES and rope_dim * 2 == LANES
    past = cache_ckv.shape[1]
    scale = float((nope + rope_dim) ** -0.5)

    pos = jnp.concatenate([jnp.arange(n_prompt, dtype=jnp.int32),
                           past + jnp.arange(t - n_prompt, dtype=jnp.int32) % dec_seq])
    cos_t, sin_t = _rope_tables(pos, rope_dim)

    hw = -(-(q_lora + kv_lora + LANES) // (2 * LANES)) * (2 * LANES)
    w_in_p = jnp.pad(w_in, ((0, 0), (0, hw - w_in.shape[1])))
    h = _matmul(xb, w_in_p, out_dtype=F32, tn=256, name="mla_in")
    cq, ckv, ckv_b, kr, krp_b = _mla_latent(h, q_g, kv_g, cos_t, sin_t, q_lora, kv_lora, rope_dim)

    w_q = jnp.pad(w_uq, ((0, 0), (0, 0), (0, 2 * LANES - nope - rope_dim))).reshape(q_lora, n_heads * 2 * LANES)
    q = _q_proj(cq, w_q, cos_t, sin_t, scale)

    kn = _matmul(ckv_b[:n_prompt], w_uk.reshape(kv_lora, n_heads * nope), out_dtype=BF16, name="k_nope")
    v = _matmul(ckv_b[:n_prompt], w_uv.reshape(kv_lora, n_heads * v_dim), out_dtype=BF16, name="v_proj")
    attn_p = _prompt_attn(q, kn, krp_b, v, n_prompt, n_heads)

    w_uk_h = jnp.transpose(w_uk, (1, 0, 2))
    w_uv_h = jnp.transpose(w_uv, (1, 0, 2))
    q_lat = _q_latent(q, w_uk_h, n_prompt)
    cache_krp = jnp.pad(cache_krope, ((0, 0), (0, 0), (0, LANES - rope_dim)))
    o_lat = _sample_attn(q_lat, cache_ckv, cache_krp, ckv_b, krp_b, n_prompt, dec_seq)
    attn_s = _o_latent(o_lat, w_uv_h)
    return jnp.concatenate([attn_p, attn_s], axis=0), ckv, kr


def kernel(x_prompt, x_sample, cache_mla_ckv, cache_mla_krope, ln_mix_g, ln_mix_b, ln_ffn_g, ln_ffn_b,
           a_w_in, a_ln_v_g, a_ln_v_b, a_w_s, a_b_s, a_w_out,
           b_w_in, b_q_norm_g, b_kv_norm_g, b_w_uq, b_w_uk, b_w_uv, b_w_out,
           router_w, router_b, exp_w_gu, exp_b_gu, exp_w_down, exp_b_down):
    depth = ln_mix_g.shape[0]
    n_mixers = 2
    alpha = float((2 * depth) ** 0.25)
    bsz, seq, d = x_prompt.shape
    dec_b, dec_seq, _ = x_sample.shape
    n_prompt = bsz * seq
    n_sample = dec_b * dec_seq
    assert bsz == 1 and n_prompt % GMLP_CHUNK == 0 and GMLP_CHUNK % dec_seq == 0

    x = jnp.concatenate([x_prompt.reshape(n_prompt, d), x_sample.reshape(n_sample, d)], axis=0)
    xb = x.astype(BF16)
    v_s, ckv_p, kr_p, ckv_s, kr_s = [], [], [], [], []
    for i in range(depth):
        j = i // n_mixers
        if i % n_mixers == 0:
            u, v = _gmlp_in(xb, a_w_in[j])
            gated, vn = _gmlp_gate(u, v, a_ln_v_g[j], a_ln_v_b[j], a_w_s[j], a_b_s[j], n_prompt, dec_seq)
            pre = _matmul(gated, a_w_out[j], out_dtype=F32, res=x, alpha=alpha, name="gmlp_out")
            v_s.append(vn[n_prompt:].reshape(dec_b, dec_seq, -1))
        else:
            attn, ckv, kr = _mla_mixer(xb, n_prompt, dec_seq, cache_mla_ckv[j], cache_mla_krope[j],
                                       b_w_in[j], b_q_norm_g[j], b_kv_norm_g[j], b_w_uq[j], b_w_uk[j],
                                       b_w_uv[j])
            pre = _matmul(attn, b_w_out[j], out_dtype=F32, res=x, alpha=alpha, name="mla_out")
            ckv_p.append(ckv[:n_prompt].reshape(bsz, seq, -1))
            kr_p.append(kr[:n_prompt].reshape(bsz, seq, -1))
            ckv_s.append(ckv[n_prompt:].reshape(dec_b, dec_seq, -1))
            kr_s.append(kr[n_prompt:].reshape(dec_b, dec_seq, -1))
        x, xb = _moe_layer(pre, ln_mix_g[i], ln_mix_b[i], router_w[i], router_b[i], exp_w_gu, exp_b_gu,
                           exp_w_down, exp_b_down, i, ln_ffn_g[i], ln_ffn_b[i], alpha)

    return (x[:n_prompt].reshape(bsz, seq, d), x[n_prompt:].reshape(dec_b, dec_seq, d),
            jnp.stack(v_s), jnp.stack(ckv_p), jnp.stack(kr_p), jnp.stack(ckv_s), jnp.stack(kr_s))
```

```python
import functools

import numpy as np
import jax
import jax.numpy as jnp
from jax import lax
from jax.experimental import pallas as pl
from jax.experimental.pallas import tpu as pltpu

F32 = jnp.float32
BF16 = jnp.bfloat16

CHUNK = 64
GMLP_CHUNK = 128
GMLP_GROUPS = 8
TOP_K = 4
ROPE_THETA = 10000.0
NEG_INF = -1e30
SWIGLU_LIMIT = 7.0
SWIGLU_ALPHA = 1.702
LN_EPS = 1e-5
RMS_EPS = 1e-6

LANES = 128
MOE_SUPER_ROWS = 1536
MOE_SUB_ROWS = 128
MOE_DOWN_TILE = 1024
DMA_UNROLL = 8
VMEM_LIMIT = 56 * 1024 * 1024


def _tile(dim, pref):
    if dim <= pref:
        return dim
    t = pref
    while dim % t:
        t //= 2
    return t


def _params(*sem):
    return pltpu.CompilerParams(dimension_semantics=sem, vmem_limit_bytes=VMEM_LIMIT)


def _layer_norm(x, g, b):
    mu = jnp.mean(x, axis=-1, keepdims=True)
    xc = x - mu
    var = jnp.mean(xc * xc, axis=-1, keepdims=True)
    return xc * lax.rsqrt(var + LN_EPS) * g + b


def _rms_norm(x, g):
    return x * lax.rsqrt(jnp.mean(x * x, axis=-1, keepdims=True) + RMS_EPS) * g


def _gelu(x):
    return 0.5 * x * (1.0 + lax.erf(x * np.float32(1.0 / np.sqrt(2.0))))


def _swap_rope_halves(x):
    half = 32
    lane = lax.broadcasted_iota(jnp.int32, x.shape, 1)
    fwd = pltpu.roll(x, LANES - half, 1)
    bwd = pltpu.roll(x, half, 1)
    return jnp.where((lane % (2 * half)) < half, fwd, bwd)


def _mm_kernel(*refs, alpha, has_res):
    if has_res:
        x_ref, w_ref, r_ref, o_ref = refs
    else:
        x_ref, w_ref, o_ref = refs
    acc = jnp.dot(x_ref[...], w_ref[...].astype(BF16), preferred_element_type=F32)
    if has_res:
        acc = alpha * r_ref[...] + acc
    o_ref[...] = acc.astype(o_ref.dtype)


def _matmul(x, w, *, out_dtype, res=None, alpha=1.0, tm=1024, tn=512, name):
    m, k = x.shape
    n = w.shape[1]
    tm, tn = _tile(m, tm), _tile(n, tn)
    in_specs = [pl.BlockSpec((tm, k), lambda i, j: (i, 0)),
                pl.BlockSpec((k, tn), lambda i, j: (0, j))]
    args = [x, w]
    if res is not None:
        in_specs.append(pl.BlockSpec((tm, tn), lambda i, j: (i, j)))
        args.append(res)
    return pl.pallas_call(
        functools.partial(_mm_kernel, alpha=alpha, has_res=res is not None),
        grid=(m // tm, n // tn),
        in_specs=in_specs,
        out_specs=pl.BlockSpec((tm, tn), lambda i, j: (i, j)),
        out_shape=jax.ShapeDtypeStruct((m, n), out_dtype),
        compiler_params=_params("parallel", "arbitrary"),
        name=name,
    )(*args)


def _gmlp_in_kernel(x_ref, wu_ref, wv_ref, u_ref, v_ref):
    x = x_ref[...]
    u_ref[...] = _gelu(jnp.dot(x, wu_ref[...].astype(BF16), preferred_element_type=F32))
    v_ref[...] = _gelu(jnp.dot(x, wv_ref[...].astype(BF16), preferred_element_type=F32))


def _gmlp_in(xb, w_in):
    m, k = xb.shape
    width = w_in.shape[1] // 2
    tm, tn = _tile(m, 1024), _tile(width, 256)
    nj = width // tn
    return pl.pallas_call(
        _gmlp_in_kernel,
        grid=(m // tm, nj),
        in_specs=[pl.BlockSpec((tm, k), lambda i, j: (i, 0)),
                  pl.BlockSpec((k, tn), lambda i, j: (0, j)),
                  pl.BlockSpec((k, tn), lambda i, j: (0, j + nj))],
        out_specs=[pl.BlockSpec((tm, tn), lambda i, j: (i, j)),
                   pl.BlockSpec((tm, tn), lambda i, j: (i, j))],
        out_shape=[jax.ShapeDtypeStruct((m, width), F32)] * 2,
        compiler_params=_params("parallel", "arbitrary"),
        name="gmlp_in",
    )(xb, w_in, w_in)


def _gmlp_gate_kernel(u_ref, v_ref, g_ref, b_ref, ws_ref, bs_ref, o_ref, vn_ref, *, n_prompt_tiles,
                      sample_shift):
    i = pl.program_id(0)
    rows = u_ref.shape[0]
    gdim = u_ref.shape[1] // GMLP_GROUPS
    vn = _layer_norm(v_ref[...], g_ref[...], b_ref[...])
    vn_ref[...] = vn
    vnb = vn.astype(BF16)
    shift = jnp.where(i >= n_prompt_tiles, sample_shift, int(np.log2(GMLP_CHUNK)))
    r = lax.broadcasted_iota(jnp.int32, (rows, rows), 0)
    c = lax.broadcasted_iota(jnp.int32, (rows, rows), 1)
    keep = (c <= r) & ((r >> shift) == (c >> shift))
    bias = bs_ref[...]
    for g in range(GMLP_GROUPS):
        w = jnp.where(keep, ws_ref[g], 0.0).astype(BF16)
        s = jnp.dot(w, vnb[:, g * gdim:(g + 1) * gdim], preferred_element_type=F32)
        s = s + bias[:, g:g + 1]
        o_ref[:, g * gdim:(g + 1) * gdim] = (u_ref[:, g * gdim:(g + 1) * gdim] * s).astype(o_ref.dtype)


def _gmlp_gate(u, v, ln_g, ln_b, w_s, b_s, n_prompt, dec_seq):
    t, width = u.shape
    rows = GMLP_CHUNK
    reps = rows // dec_seq
    w_prompt = w_s
    w_sample = jnp.tile(w_s[:, :dec_seq, :dec_seq], (1, reps, reps))
    b_prompt = b_s.T
    b_sample = jnp.tile(b_s[:, :dec_seq].T, (reps, 1))
    w_all = jnp.stack([w_prompt, w_sample])
    b_all = jnp.stack([b_prompt, b_sample])
    n_prompt_tiles = n_prompt // rows
    sel = lambda i: (i >= n_prompt_tiles).astype(jnp.int32)
    return pl.pallas_call(
        functools.partial(_gmlp_gate_kernel, n_prompt_tiles=n_prompt_tiles,
                          sample_shift=int(np.log2(dec_seq))),
        grid=(t // rows,),
        in_specs=[pl.BlockSpec((rows, width), lambda i: (i, 0)),
                  pl.BlockSpec((rows, width), lambda i: (i, 0)),
                  pl.BlockSpec((1, width), lambda i: (0, 0)),
                  pl.BlockSpec((1, width), lambda i: (0, 0)),
                  pl.BlockSpec((None, GMLP_GROUPS, rows, rows), lambda i: (sel(i), 0, 0, 0)),
                  pl.BlockSpec((None, rows, GMLP_GROUPS), lambda i: (sel(i), 0, 0))],
        out_specs=[pl.BlockSpec((rows, width), lambda i: (i, 0)),
                   pl.BlockSpec((rows, width), lambda i: (i, 0))],
        out_shape=[jax.ShapeDtypeStruct((t, width), BF16), jax.ShapeDtypeStruct((t, width), F32)],
        compiler_params=_params("parallel"),
        name="gmlp_gate",
    )(u, v, ln_g.reshape(1, -1), ln_b.reshape(1, -1), w_all, b_all)


def _ln_router_kernel(p_ref, g_ref, b_ref, rw_ref, rb_ref, y_ref, pk_ref, idx_ref, rank_ref, gate_ref,
                      count_ref, cnt_ref):
    y = _layer_norm(p_ref[...], g_ref[...], b_ref[...])
    y_ref[...] = y
    rw = rw_ref[...]
    y_hi, w_hi = y.astype(BF16), rw.astype(BF16)
    half = y.shape[1] // 2
    y_bits = lax.bitcast_convert_type(y_hi.astype(F32), jnp.uint32)
    pk_ref[...] = (y_bits[:, :half] >> 16) | (y_bits[:, half:] & jnp.uint32(0xFFFF0000))
    y_lo = (y - y_hi.astype(F32)).astype(BF16)
    w_lo = (rw - w_hi.astype(F32)).astype(BF16)
    logits = (jnp.dot(y_hi, w_hi, preferred_element_type=F32)
              + (jnp.dot(y_lo, w_hi, preferred_element_type=F32)
                 + jnp.dot(y_hi, w_lo, preferred_element_type=F32))) + rb_ref[...]
    n_e = logits.shape[1]
    col = lax.broadcasted_iota(jnp.int32, logits.shape, 1)
    vals, idxs = [], []
    cur = logits
    for _ in range(TOP_K):
        mx = jnp.max(cur, axis=-1, keepdims=True)
        am = jnp.min(jnp.where(cur == mx, col, n_e), axis=-1, keepdims=True)
        vals.append(mx)
        idxs.append(am)
        cur = jnp.where(col == am, -jnp.inf, cur)
    es = [jnp.exp(v - vals[0]) for v in vals]
    denom = es[0]
    for e in es[1:]:
        denom = denom + e
    tm = logits.shape[0]

    @pl.when(pl.program_id(0) == 0)
    def _():
        cnt_ref[...] = jnp.zeros_like(cnt_ref)
    onehots = [(col == am).astype(F32) for am in idxs]
    tile_hot = onehots[0]
    for oh in onehots[1:]:
        tile_hot = tile_hot + oh
    lower = (lax.broadcasted_iota(jnp.int32, (tm, tm), 1)
             < lax.broadcasted_iota(jnp.int32, (tm, tm), 0)).astype(BF16)
    before = jnp.dot(lower, tile_hot.astype(BF16), preferred_element_type=F32) + cnt_ref[...]
    cnt_ref[...] = cnt_ref[...] + jnp.sum(tile_hot, axis=0, keepdims=True)
    count_ref[...] = cnt_ref[...]

    kcol = lax.broadcasted_iota(jnp.int32, (tm, TOP_K), 1)
    idx_out = jnp.zeros((tm, TOP_K), jnp.int32)
    rank_out = jnp.zeros((tm, TOP_K), jnp.int32)
    gate_out = jnp.zeros((tm, TOP_K), F32)
    for k in range(TOP_K):
        rank_k = jnp.sum(onehots[k] * before, axis=-1, keepdims=True).astype(jnp.int32)
        idx_out = jnp.where(kcol == k, idxs[k], idx_out)
        rank_out = jnp.where(kcol == k, rank_k, rank_out)
        gate_out = jnp.where(kcol == k, es[k] / denom, gate_out)
    idx_ref[...] = idx_out
    rank_ref[...] = rank_out
    gate_ref[...] = gate_out


def _ln_router(pre, g, b, rw, rb):
    t, d = pre.shape
    n_e = rw.shape[1]
    tm = _tile(t, 256)
    return pl.pallas_call(
        _ln_router_kernel,
        grid=(t // tm,),
        in_specs=[pl.BlockSpec((tm, d), lambda i: (i, 0)),
                  pl.BlockSpec((1, d), lambda i: (0, 0)),
                  pl.BlockSpec((1, d), lambda i: (0, 0)),
                  pl.BlockSpec((d, n_e), lambda i: (0, 0)),
                  pl.BlockSpec((1, n_e), lambda i: (0, 0))],
        out_specs=[pl.BlockSpec((tm, d), lambda i: (i, 0)),
                   pl.BlockSpec((tm, d // 2), lambda i: (i, 0)),
                   pl.BlockSpec((tm, TOP_K), lambda i: (i, 0)),
                   pl.BlockSpec((tm, TOP_K), lambda i: (i, 0)),
                   pl.BlockSpec((tm, TOP_K), lambda i: (i, 0)),
                   pl.BlockSpec((1, n_e), lambda i: (0, 0))],
        out_shape=[jax.ShapeDtypeStruct((t, d), F32),
                   jax.ShapeDtypeStruct((t, d // 2), jnp.uint32),
                   jax.ShapeDtypeStruct((t, TOP_K), jnp.int32),
                   jax.ShapeDtypeStruct((t, TOP_K), jnp.int32),
                   jax.ShapeDtypeStruct((t, TOP_K), F32),
                   jax.ShapeDtypeStruct((1, n_e), F32)],
        scratch_shapes=[pltpu.VMEM((1, n_e), F32)],
        compiler_params=_params("arbitrary"),
        name="ln_router",
    )(pre, g.reshape(1, -1), b.reshape(1, -1), rw, rb.reshape(1, -1))


def _route_tables(top_idx, rank, counts, n_e, n_sb):
    t = top_idx.shape[0]
    flat_e = top_idx.reshape(-1)
    rank = rank.reshape(-1)
    counts = counts.reshape(-1).astype(jnp.int32)
    sb_per_e = (counts + MOE_SUPER_ROWS - 1) // MOE_SUPER_ROWS
    per_sb = -(-counts // jnp.maximum(sb_per_e, 1))
    per_sb = jnp.maximum((per_sb + MOE_SUB_ROWS - 1) // MOE_SUB_ROWS * MOE_SUB_ROWS, MOE_SUB_ROWS)
    sb_end = jnp.cumsum(sb_per_e)
    sb_first = sb_end - sb_per_e
    n_used = sb_end[-1]
    sb = jnp.arange(n_sb, dtype=jnp.int32)
    sb_c = jnp.minimum(sb, n_used - 1)
    e_of = jnp.minimum(jnp.searchsorted(sb_end, sb_c, side="right"), n_e - 1).astype(jnp.int32)
    rows = jnp.clip(counts[e_of] - (sb_c - sb_first[e_of]) * per_sb[e_of], 0, per_sb[e_of])
    rows = jnp.where(sb < n_used, rows, 0).astype(jnp.int32)
    a_sb = sb_first[flat_e] + rank // per_sb[flat_e]
    dest = (a_sb * MOE_SUPER_ROWS + rank % per_sb[flat_e]).astype(jnp.int32)
    tok = (jnp.arange(t * TOP_K, dtype=jnp.int32) // TOP_K)
    slot_tok = jnp.zeros((n_sb * MOE_SUPER_ROWS,), jnp.int32).at[dest].set(tok)
    return e_of, rows, slot_tok, dest


def _row_blocks(rows, compute, zero):
    sub = MOE_SUB_ROWS
    nb = (rows + sub - 1) // sub
    n_big = nb // 4
    rem = nb % 4

    def big(s, c):
        compute(pl.multiple_of(s * 4 * sub, 4 * sub), 4 * sub)
        return c
    lax.fori_loop(0, n_big, big, 0)

    @pl.when(rem >= 2)
    def _():
        compute(pl.multiple_of(n_big * 4 * sub, 2 * sub), 2 * sub)

    @pl.when(rem % 2 == 1)
    def _():
        compute(pl.multiple_of((nb - 1) * sub, sub), sub)

    def zero_body(s, c):
        zero(pl.multiple_of(s * sub, sub))
        return c
    lax.fori_loop(nb, MOE_SUPER_ROWS // sub, zero_body, 0)


def _moe_up_kernel(sb_e, sb_rows, slot_tok, y_hbm, wg_ref, wu_ref, bg_ref, bu_ref, act_ref,
                   xpk, xb16, sem, *, n_sb, nj):
    sb = pl.program_id(0)
    j = pl.program_id(1)
    tn = wg_ref.shape[1]
    sub = MOE_SUB_ROWS
    per_sub = sub // nj
    rows = sb_rows[sb]

    def padded(r, m):
        return (r + m - 1) // m * m

    p_cur = padded(rows, sub)
    base_next = jnp.minimum(sb + 1, n_sb - 1) * MOE_SUPER_ROWS

    def row_copy(tok, r):
        return pltpu.make_async_copy(y_hbm.at[pl.ds(tok, 1)], xpk.at[pl.ds(r, 1)], sem.at[0])

    def wait_rows(n):
        n = pl.multiple_of(n, DMA_UNROLL)
        pltpu.make_async_copy(y_hbm.at[pl.ds(0, n)], xpk.at[pl.ds(0, n)], sem.at[0]).wait()

    def fetch_loop(sb_i, lo, hi):
        base = sb_i * MOE_SUPER_ROWS

        def body_grp(g, c):
            r0 = g * DMA_UNROLL
            for u in range(DMA_UNROLL):
                row_copy(slot_tok[base + r0 + u], r0 + u).start()
            return c
        lax.fori_loop(lo // DMA_UNROLL, hi // DMA_UNROLL, body_grp, 0)

    @pl.when((sb == 0) & (j == 0))
    def _():
        xpk[...] = jnp.zeros_like(xpk)
        fetch_loop(0, 0, padded(sb_rows[0], DMA_UNROLL))

    @pl.when(j == 0)
    def _():
        p_prev = jnp.where(sb > 0, padded(sb_rows[jnp.maximum(sb - 1, 0)], sub), 0)
        n_wait = jnp.maximum(p_prev, padded(rows, DMA_UNROLL))

        @pl.when(n_wait > 0)
        def _():
            wait_rows(n_wait)
        half = xpk.shape[1]
        pk = xpk[...]
        xb16[:, :half] = lax.bitcast_convert_type(pk << 16, F32).astype(BF16)
        xb16[:, half:] = lax.bitcast_convert_type(pk & jnp.uint32(0xFFFF0000), F32).astype(BF16)

        @pl.when(sb + 1 < n_sb)
        def _():
            fetch_loop(sb + 1, p_cur, padded(sb_rows[sb + 1], DMA_UNROLL))

    def compute(r0, n):
        k0 = j * (p_cur // nj) + (r0 // sub) * per_sub
        for u in range((n // sub) * per_sub):
            row_copy(slot_tok[base_next + k0 + u], k0 + u).start()
        xs = xb16[pl.ds(r0, n), :]
        hg = jnp.dot(xs, wg_ref[...].astype(BF16), preferred_element_type=F32)
        hu = jnp.dot(xs, wu_ref[...].astype(BF16), preferred_element_type=F32)
        gate = jnp.minimum(hg + bg_ref[...], SWIGLU_LIMIT)
        up = jnp.clip(hu + bu_ref[...], -SWIGLU_LIMIT, SWIGLU_LIMIT)
        a = (up + 1.0) * gate * jax.nn.sigmoid(SWIGLU_ALPHA * gate)
        act_ref[pl.ds(r0, n), :] = a.astype(act_ref.dtype)

    def zero(r0):
        act_ref[pl.ds(r0, MOE_SUB_ROWS), :] = jnp.zeros((MOE_SUB_ROWS, tn), act_ref.dtype)

    _row_blocks(rows, compute, zero)

    @pl.when((sb == n_sb - 1) & (j == nj - 1) & (p_cur > 0))
    def _():
        wait_rows(p_cur)


def _moe_up(y_pk, w_gu, b_gu, layer, sb_e, sb_rows, slot_tok):
    _, n_e, d, ff2 = w_gu.shape
    assert y_pk.shape[1] * 2 == d
    ff = ff2 // 2
    n_sb = sb_e.shape[0]
    tn = _tile(ff, 256)
    nj = ff // tn
    assert MOE_SUB_ROWS % nj == 0 and MOE_SUPER_ROWS % (4 * MOE_SUB_ROWS) == 0
    b4 = b_gu.reshape(b_gu.shape[0], n_e, 1, ff2)

    def jw(sb, j, rows):
        return jnp.where(rows[sb] > 0, j, nj - 1)

    grid_spec = pltpu.PrefetchScalarGridSpec(
        num_scalar_prefetch=3,
        grid=(n_sb, nj),
        in_specs=[pl.BlockSpec(memory_space=pl.ANY),
                  pl.BlockSpec((None, None, d, tn), lambda sb, j, e, rows, st: (layer, e[sb], 0, jw(sb, j, rows))),
                  pl.BlockSpec((None, None, d, tn), lambda sb, j, e, rows, st: (layer, e[sb], 0, jw(sb, j, rows) + nj)),
                  pl.BlockSpec((None, None, 1, tn), lambda sb, j, e, rows, st: (layer, e[sb], 0, jw(sb, j, rows))),
                  pl.BlockSpec((None, None, 1, tn), lambda sb, j, e, rows, st: (layer, e[sb], 0, jw(sb, j, rows) + nj))],
        out_specs=pl.BlockSpec((MOE_SUPER_ROWS, tn), lambda sb, j, e, rows, st: (sb, j)),
        scratch_shapes=[pltpu.VMEM((MOE_SUPER_ROWS, d // 2), jnp.uint32),
                        pltpu.VMEM((MOE_SUPER_ROWS, d), BF16),
                        pltpu.SemaphoreType.DMA((1,))],
    )
    return pl.pallas_call(
        functools.partial(_moe_up_kernel, n_sb=n_sb, nj=nj),
        grid_spec=grid_spec,
        out_shape=jax.ShapeDtypeStruct((n_sb * MOE_SUPER_ROWS, ff), BF16),
        compiler_params=_params("arbitrary", "arbitrary"),
        name="moe_up",
    )(sb_e, sb_rows, slot_tok, y_pk, w_gu, w_gu, b4, b4)


def _moe_down_kernel(sb_e, sb_rows, a_ref, w_ref, b_ref, o_ref):
    sb = pl.program_id(0)
    tn = w_ref.shape[1]
    rows = sb_rows[sb]

    def compute(r0, n):
        o = jnp.dot(a_ref[pl.ds(r0, n), :], w_ref[...].astype(BF16), preferred_element_type=F32) + b_ref[...]
        bits = lax.bitcast_convert_type(o.astype(BF16).astype(F32), jnp.uint32)
        o_ref[pl.ds(r0, n), :] = (bits[:, :tn // 2] >> 16) | (bits[:, tn // 2:] & jnp.uint32(0xFFFF0000))

    def zero(r0):
        o_ref[pl.ds(r0, MOE_SUB_ROWS), :] = jnp.zeros((MOE_SUB_ROWS, tn // 2), jnp.uint32)

    _row_blocks(rows, compute, zero)


def _moe_down(act, w_down, b_down, layer, sb_e, sb_rows):
    n_slots, ff = act.shape
    _, n_e, _, d = w_down.shape
    n_sb = sb_e.shape[0]
    tn = _tile(d, MOE_DOWN_TILE)
    nj = d // tn
    b4 = b_down.reshape(b_down.shape[0], n_e, 1, d)

    def jw(sb, j, rows):
        return jnp.where(rows[sb] > 0, j, nj - 1)

    grid_spec = pltpu.PrefetchScalarGridSpec(
        num_scalar_prefetch=2,
        grid=(n_sb, nj),
        in_specs=[pl.BlockSpec((MOE_SUPER_ROWS, ff), lambda sb, j, e, rows: (sb, 0)),
                  pl.BlockSpec((None, None, ff, tn), lambda sb, j, e, rows: (layer, e[sb], 0, jw(sb, j, rows))),
                  pl.BlockSpec((None, None, 1, tn), lambda sb, j, e, rows: (layer, e[sb], 0, jw(sb, j, rows)))],
        out_specs=pl.BlockSpec((MOE_SUPER_ROWS, tn // 2), lambda sb, j, e, rows: (sb, j)),
    )
    return pl.pallas_call(
        _moe_down_kernel,
        grid_spec=grid_spec,
        out_shape=jax.ShapeDtypeStruct((n_slots, d // 2), jnp.uint32),
        compiler_params=_params("arbitrary", "arbitrary"),
        name="moe_down",
    )(sb_e, sb_rows, act, w_down, b4)


def _moe_combine_kernel(dest, ys_hbm, gate_ref, y_ref, g_ref, b_ref, o_ref, ob_ref, gbuf, sem, *, alpha):
    i = pl.program_id(0)
    n = pl.num_programs(0)
    tm = y_ref.shape[0]
    slot = i % 2

    def issue(tile, s):
        base = tile * tm * TOP_K

        def body(r, c):
            for k in range(TOP_K):
                pltpu.make_async_copy(ys_hbm.at[pl.ds(dest[base + r * TOP_K + k], 1)],
                                      gbuf.at[s, k, pl.ds(r, 1)], sem.at[s]).start()
            return c
        lax.fori_loop(0, tm, body, 0, unroll=DMA_UNROLL // TOP_K)

    @pl.when(i == 0)
    def _():
        issue(0, 0)

    @pl.when(i + 1 < n)
    def _():
        issue(i + 1, 1 - slot)

    for k in range(TOP_K):
        pltpu.make_async_copy(ys_hbm.at[pl.ds(0, tm)], gbuf.at[slot, k], sem.at[slot]).wait()

    gates = gate_ref[...]
    d = y_ref.shape[1]
    tile = min(MOE_DOWN_TILE, d)
    pieces = []
    for jt in range(d // tile):
        lo_sum = hi_sum = None
        for k in range(TOP_K):
            w = gbuf[slot, k, :, jt * (tile // 2):(jt + 1) * (tile // 2)]
            lo = gates[:, k:k + 1] * lax.bitcast_convert_type(w << 16, F32)
            hi = gates[:, k:k + 1] * lax.bitcast_convert_type(w & jnp.uint32(0xFFFF0000), F32)
            lo_sum = lo if k == 0 else lo_sum + lo
            hi_sum = hi if k == 0 else hi_sum + hi
        pieces += [lo_sum, hi_sum]
    moe = jnp.concatenate(pieces, axis=1)
    out = _layer_norm(alpha * y_ref[...] + moe, g_ref[...], b_ref[...])
    o_ref[...] = out
    ob_ref[...] = out.astype(ob_ref.dtype)


def _moe_combine(ys, dest, gates, y, g, b, alpha):
    t, d = y.shape
    tm = _tile(t, 128)
    grid_spec = pltpu.PrefetchScalarGridSpec(
        num_scalar_prefetch=1,
        grid=(t // tm,),
        in_specs=[pl.BlockSpec(memory_space=pl.ANY),
                  pl.BlockSpec((tm, TOP_K), lambda i, dst: (i, 0)),
                  pl.BlockSpec((tm, d), lambda i, dst: (i, 0)),
                  pl.BlockSpec((1, d), lambda i, dst: (0, 0)),
                  pl.BlockSpec((1, d), lambda i, dst: (0, 0))],
        out_specs=[pl.BlockSpec((tm, d), lambda i, dst: (i, 0)),
                   pl.BlockSpec((tm, d), lambda i, dst: (i, 0))],
        scratch_shapes=[pltpu.VMEM((2, TOP_K, tm, d // 2), jnp.uint32),
                        pltpu.SemaphoreType.DMA((2,))],
    )
    return pl.pallas_call(
        functools.partial(_moe_combine_kernel, alpha=alpha),
        grid_spec=grid_spec,
        out_shape=[jax.ShapeDtypeStruct((t, d), F32), jax.ShapeDtypeStruct((t, d), BF16)],
        compiler_params=_params("arbitrary"),
        name="moe_combine",
    )(dest, ys, gates, y, g.reshape(1, -1), b.reshape(1, -1))


def _moe_layer(pre, ln_g, ln_b, rw, rb, w_gu, b_gu, w_down, b_down, layer, ffn_g, ffn_b, alpha):
    t = pre.shape[0]
    n_e = rw.shape[1]
    y, y_pk, top_idx, rank, gates, counts = _ln_router(pre, ln_g, ln_b, rw, rb)
    n_sb = (t * TOP_K) // MOE_SUPER_ROWS + n_e
    sb_e, sb_rows, slot_tok, dest = _route_tables(top_idx, rank, counts, n_e, n_sb)
    act = _moe_up(y_pk, w_gu, b_gu, layer, sb_e, sb_rows, slot_tok)
    ys = _moe_down(act, w_down, b_down, layer, sb_e, sb_rows)
    return _moe_combine(ys, dest, gates, y, ffn_g, ffn_b, alpha)


def _mla_latent_kernel(h_ref, qg_ref, kvg_ref, cos_ref, sin_ref, cq_ref, ckv_ref, ckvb_ref, kr_ref,
                       krp_ref, *, q_lora, kv_lora, rope_dim):
    h = h_ref[...]
    cq_ref[...] = _rms_norm(h[:, :q_lora], qg_ref[...]).astype(cq_ref.dtype)
    ckv = _rms_norm(h[:, q_lora:q_lora + kv_lora], kvg_ref[...])
    ckv_ref[...] = ckv
    ckvb_ref[...] = ckv.astype(ckvb_ref.dtype)
    x = h[:, q_lora + kv_lora:q_lora + kv_lora + LANES]
    kr = x * cos_ref[...] + _swap_rope_halves(x) * sin_ref[...]
    kr_ref[...] = kr[:, :rope_dim]
    krp_ref[...] = kr.astype(krp_ref.dtype)


def _mla_latent(h, q_g, kv_g, cos_t, sin_t, q_lora, kv_lora, rope_dim):
    t, hw = h.shape
    tm = _tile(t, 512)
    return pl.pallas_call(
        functools.partial(_mla_latent_kernel, q_lora=q_lora, kv_lora=kv_lora, rope_dim=rope_dim),
        grid=(t // tm,),
        in_specs=[pl.BlockSpec((tm, hw), lambda i: (i, 0)),
                  pl.BlockSpec((1, q_lora), lambda i: (0, 0)),
                  pl.BlockSpec((1, kv_lora), lambda i: (0, 0)),
                  pl.BlockSpec((tm, LANES), lambda i: (i, 0)),
                  pl.BlockSpec((tm, LANES), lambda i: (i, 0))],
        out_specs=[pl.BlockSpec((tm, q_lora), lambda i: (i, 0)),
                   pl.BlockSpec((tm, kv_lora), lambda i: (i, 0)),
                   pl.BlockSpec((tm, kv_lora), lambda i: (i, 0)),
                   pl.BlockSpec((tm, rope_dim), lambda i: (i, 0)),
                   pl.BlockSpec((tm, LANES), lambda i: (i, 0))],
        out_shape=[jax.ShapeDtypeStruct((t, q_lora), BF16),
                   jax.ShapeDtypeStruct((t, kv_lora), F32),
                   jax.ShapeDtypeStruct((t, kv_lora), BF16),
                   jax.ShapeDtypeStruct((t, rope_dim), F32),
                   jax.ShapeDtypeStruct((t, LANES), BF16)],
        compiler_params=_params("parallel"),
        name="mla_latent",
    )(h, q_g.reshape(1, -1), kv_g.reshape(1, -1), cos_t, sin_t)


def _q_proj_kernel(x_ref, w_ref, cos_ref, sin_ref, o_ref, *, scale):
    acc = jnp.dot(x_ref[...], w_ref[...].astype(BF16), preferred_element_type=F32) * scale
    cos, sin = cos_ref[...], sin_ref[...]
    for g in range(acc.shape[1] // LANES):
        blk = acc[:, g * LANES:(g + 1) * LANES]
        if g % 2:
            blk = blk * cos + _swap_rope_halves(blk) * sin
        o_ref[:, g * LANES:(g + 1) * LANES] = blk.astype(o_ref.dtype)


def _q_proj(cq, w_q, cos_t, sin_t, scale):
    t, k = cq.shape
    n = w_q.shape[1]
    tm, tn = _tile(t, 1024), _tile(n, 512)
    return pl.pallas_call(
        functools.partial(_q_proj_kernel, scale=scale),
        grid=(t // tm, n // tn),
        in_specs=[pl.BlockSpec((tm, k), lambda i, j: (i, 0)),
                  pl.BlockSpec((k, tn), lambda i, j: (0, j)),
                  pl.BlockSpec((tm, LANES), lambda i, j: (i, 0)),
                  pl.BlockSpec((tm, LANES), lambda i, j: (i, 0))],
        out_specs=pl.BlockSpec((tm, tn), lambda i, j: (i, j)),
        out_shape=jax.ShapeDtypeStruct((t, n), BF16),
        compiler_params=_params("parallel", "arbitrary"),
        name="q_proj",
    )(cq, w_q, cos_t, sin_t)


def _prompt_attn_kernel(q_ref, kn_ref, kr_ref, v_ref, o_ref, kcat, vcat, s_e0, s_e1, s_o0, s_o1,
                        m_0, m_1, acc_0, acc_1, *, tq, tk):
    qi = pl.program_id(1)
    hd = 2 * LANES
    s_even, s_odd = (s_e0, s_e1), (s_o0, s_o1)
    m_refs, acc_refs = (m_0, m_1), (acc_0, acc_1)

    @pl.when(qi == 0)
    def _():
        for a in range(2):
            kcat[a, :, :LANES] = kn_ref[:, a * LANES:(a + 1) * LANES]
            kcat[a, :, LANES:] = kr_ref[...]
            vcat[a, :, :LANES] = v_ref[:, a * LANES:(a + 1) * LANES]
            vcat[a, :, LANES:] = jnp.ones((vcat.shape[1], LANES), vcat.dtype)

    for a in range(2):
        m_refs[a][...] = jnp.full(m_refs[a].shape, NEG_INF, F32)
        acc_refs[a][...] = jnp.zeros(acc_refs[a].shape, F32)

    def scores(kb, bufs, r_lo=0):
        k0 = pl.multiple_of(kb * tk, tk)
        for a in range(2):
            bufs[a][r_lo:, :] = lax.dot_general(q_ref[r_lo:, a * hd:(a + 1) * hd], kcat[a, pl.ds(k0, tk), :],
                                                (((1,), (1,)), ((), ())), preferred_element_type=F32)

    def softmax_pv(kb, bufs, masked, r_lo=0):
        k0 = pl.multiple_of(kb * tk, tk)
        if masked:
            r = qi * tq + r_lo + lax.broadcasted_iota(jnp.int32, (tq - r_lo, tk), 0)
            c = k0 + lax.broadcasted_iota(jnp.int32, (tq - r_lo, tk), 1)
            ok = (c // CHUNK) <= (r // CHUNK)
        for a in range(2):
            s = bufs[a][r_lo:, :]
            if masked:
                s = jnp.where(ok, s, NEG_INF)
            m_prev = m_refs[a][r_lo:, :]
            m_new = jnp.maximum(m_prev, jnp.max(s, axis=-1, keepdims=True))
            p = jnp.exp(s - m_new).astype(BF16)
            acc_refs[a][r_lo:, :] = jnp.exp(m_prev - m_new) * acc_refs[a][r_lo:, :] + jnp.dot(
                p, vcat[a, pl.ds(k0, tk), :], preferred_element_type=F32)
            m_refs[a][r_lo:, :] = m_new

    n_pair = (qi * tq) // (2 * tk)
    scores(0, s_even)

    def body(i, carry):
        kb = 2 * i
        scores(kb + 1, s_odd)
        softmax_pv(kb, s_even, False)
        scores(kb + 2, s_even)
        softmax_pv(kb + 1, s_odd, False)
        return carry
    lax.fori_loop(0, n_pair, body, 0)
    kb = 2 * n_pair
    scores(kb + 1, s_odd, tk)
    softmax_pv(kb, s_even, True)
    softmax_pv(kb + 1, s_odd, True, tk)
    for a in range(2):
        o_ref[:, a * LANES:(a + 1) * LANES] = (
            acc_refs[a][:, :LANES] / acc_refs[a][:, LANES:]).astype(o_ref.dtype)


def _prompt_attn(q, kn, krp, v, n_prompt, n_heads):
    tq = _tile(n_prompt, 1024)
    tk = tq // 2
    assert tk % CHUNK == 0 and n_prompt % tq == 0
    return pl.pallas_call(
        functools.partial(_prompt_attn_kernel, tq=tq, tk=tk),
        grid=(n_heads // 2, n_prompt // tq),
        in_specs=[pl.BlockSpec((tq, 4 * LANES), lambda p, i: (i, p)),
                  pl.BlockSpec((n_prompt, 2 * LANES), lambda p, i: (0, p), pipeline_mode=pl.Buffered(1)),
                  pl.BlockSpec((n_prompt, LANES), lambda p, i: (0, 0), pipeline_mode=pl.Buffered(1)),
                  pl.BlockSpec((n_prompt, 2 * LANES), lambda p, i: (0, p), pipeline_mode=pl.Buffered(1))],
        out_specs=pl.BlockSpec((tq, 2 * LANES), lambda p, i: (i, p)),
        out_shape=jax.ShapeDtypeStruct((n_prompt, n_heads * LANES), BF16),
        scratch_shapes=([pltpu.VMEM((2, n_prompt, 2 * LANES), BF16)] * 2
                        + [pltpu.VMEM((tq, tk), F32)] * 4
                        + [pltpu.VMEM((tq, 1), F32)] * 2
                        + [pltpu.VMEM((tq, 2 * LANES), F32)] * 2),
        compiler_params=_params("arbitrary", "arbitrary"),
        name="prompt_attn",
    )(q, kn, krp, v)


def _q_latent_kernel(q_ref, wuk_ref, o_ref, *, kv_lora):
    qn = q_ref[:, :LANES]
    q_lat = lax.dot_general(qn, wuk_ref[...].astype(BF16), (((1,), (1,)), ((), ())),
                            preferred_element_type=F32)
    o_ref[:, :kv_lora] = q_lat.astype(o_ref.dtype)
    o_ref[:, kv_lora:] = q_ref[:, LANES:]


def _q_latent(q, w_uk_h, n_prompt):
    n_heads, kv_lora, _ = w_uk_h.shape
    n_s = q.shape[0] - n_prompt
    assert n_prompt % n_s == 0
    rb = n_prompt // n_s
    return pl.pallas_call(
        functools.partial(_q_latent_kernel, kv_lora=kv_lora),
        grid=(n_heads,),
        in_specs=[pl.BlockSpec((n_s, 2 * LANES), lambda h: (rb, h)),
                  pl.BlockSpec((None, kv_lora, LANES), lambda h: (h, 0, 0))],
        out_specs=pl.BlockSpec((None, n_s, kv_lora + LANES), lambda h: (h, 0, 0)),
        out_shape=jax.ShapeDtypeStruct((n_heads, n_s, kv_lora + LANES), BF16),
        compiler_params=_params("parallel"),
        name="q_latent",
    )(q, w_uk_h)


def _sample_attn_kernel(q_ref, cc_ref, ck_ref, nc_ref, nk_ref, o_ref, kcat, *, past, dec_seq, kv_lora,
                        n_keys_pad):
    n_heads = q_ref.shape[0]
    n_q = n_heads * dec_seq
    kw = kv_lora + LANES
    kcat[0:past, 0:kv_lora] = cc_ref[...].astype(BF16)
    kcat[0:past, kv_lora:kw] = ck_ref[...].astype(BF16)
    kcat[past:past + dec_seq, 0:kv_lora] = nc_ref[...]
    kcat[past:past + dec_seq, kv_lora:kw] = nk_ref[...]
    kcat[past + dec_seq:n_keys_pad, :] = jnp.zeros((n_keys_pad - past - dec_seq, kw), BF16)
    q = q_ref[...].reshape(n_q, kw)
    s = lax.dot_general(q, kcat[...], (((1,), (1,)), ((), ())), preferred_element_type=F32)
    q_pos = past + lax.broadcasted_iota(jnp.int32, s.shape, 0) % dec_seq
    k_pos = lax.broadcasted_iota(jnp.int32, s.shape, 1)
    ok = ((k_pos // CHUNK) <= (q_pos // CHUNK)) & (k_pos < past + dec_seq)
    s = jnp.where(ok, s, NEG_INF)
    p = jnp.exp(s - jnp.max(s, axis=-1, keepdims=True))
    p = p / jnp.sum(p, axis=-1, keepdims=True)
    o = jnp.dot(p.astype(BF16), kcat[:, 0:kv_lora], preferred_element_type=F32)
    o_ref[...] = o.astype(o_ref.dtype).reshape(n_heads, dec_seq, kv_lora)


def _sample_attn(q_lat, cache_ckv, cache_krp, ckv_b, krp_b, row0, dec_seq):
    n_heads, n_s, kw = q_lat.shape
    n_b, past, kv_lora = cache_ckv.shape
    n_keys_pad = -(-(past + dec_seq) // LANES) * LANES
    rb0 = row0 // dec_seq
    return pl.pallas_call(
        functools.partial(_sample_attn_kernel, past=past, dec_seq=dec_seq, kv_lora=kv_lora,
                          n_keys_pad=n_keys_pad),
        grid=(n_b,),
        in_specs=[pl.BlockSpec((n_heads, dec_seq, kw), lambda b: (0, b, 0)),
                  pl.BlockSpec((None, past, kv_lora), lambda b: (b, 0, 0)),
                  pl.BlockSpec((None, past, LANES), lambda b: (b, 0, 0)),
                  pl.BlockSpec((dec_seq, kv_lora), lambda b: (rb0 + b, 0)),
                  pl.BlockSpec((dec_seq, LANES), lambda b: (rb0 + b, 0))],
        out_specs=pl.BlockSpec((n_heads, dec_seq, kv_lora), lambda b: (0, b, 0)),
        out_shape=jax.ShapeDtypeStruct((n_heads, n_s, kv_lora), BF16),
        scratch_shapes=[pltpu.VMEM((n_keys_pad, kw), BF16)],
        compiler_params=_params("parallel"),
        name="sample_attn",
    )(q_lat, cache_ckv, cache_krp, ckv_b, krp_b)


def _o_latent_kernel(o_ref, wuv_ref, out_ref):
    out_ref[...] = jnp.dot(o_ref[...], wuv_ref[...].astype(BF16),
                           preferred_element_type=F32).astype(out_ref.dtype)


def _o_latent(o_lat, w_uv_h):
    n_heads, n_s, kv_lora = o_lat.shape
    return pl.pallas_call(
        _o_latent_kernel,
        grid=(n_heads,),
        in_specs=[pl.BlockSpec((None, n_s, kv_lora), lambda h: (h, 0, 0)),
                  pl.BlockSpec((None, kv_lora, LANES), lambda h: (h, 0, 0))],
        out_specs=pl.BlockSpec((n_s, LANES), lambda h: (0, h)),
        out_shape=jax.ShapeDtypeStruct((n_s, n_heads * LANES), BF16),
        compiler_params=_params("parallel"),
        name="o_latent",
    )(o_lat, w_uv_h)


def _rope_tables(pos, rope_dim):
    half = rope_dim // 2
    inv_freq = ROPE_THETA ** (-jnp.arange(half, dtype=F32) / half)
    ang = pos.astype(F32)[:, None] * inv_freq[None, :]
    cos, sin = jnp.cos(ang), jnp.sin(ang)
    zeros = jnp.zeros((pos.shape[0], LANES - rope_dim), F32)
    return (jnp.concatenate([cos, cos, zeros], axis=1), jnp.concatenate([-sin, sin, zeros], axis=1))


def _mla_mixer(xb, n_prompt, dec_seq, cache_ckv, cache_krope, w_in, q_g, kv_g, w_uq, w_uk, w_uv):
    t = xb.shape[0]
    q_lora = q_g.shape[0]
    kv_lora, n_heads, nope = w_uk.shape
    rope_dim = w_uq.shape[2] - nope
    v_dim = w_uv.shape[2]
    assert nope == LANES and v_dim == LANES and rope_dim * 2 == LANES
    past = cache_ckv.shape[1]
    scale = float((nope + rope_dim) ** -0.5)

    pos = jnp.concatenate([jnp.arange(n_prompt, dtype=jnp.int32),
                           past + jnp.arange(t - n_prompt, dtype=jnp.int32) % dec_seq])
    cos_t, sin_t = _rope_tables(pos, rope_dim)

    hw = -(-(q_lora + kv_lora + LANES) // (2 * LANES)) * (2 * LANES)
    w_in_p = jnp.pad(w_in, ((0, 0), (0, hw - w_in.shape[1])))
    h = _matmul(xb, w_in_p, out_dtype=F32, tn=256, name="mla_in")
    cq, ckv, ckv_b, kr, krp_b = _mla_latent(h, q_g, kv_g, cos_t, sin_t, q_lora, kv_lora, rope_dim)

    w_q = jnp.pad(w_uq, ((0, 0), (0, 0), (0, 2 * LANES - nope - rope_dim))).reshape(q_lora, n_heads * 2 * LANES)
    q = _q_proj(cq, w_q, cos_t, sin_t, scale)

    kn = _matmul(ckv_b[:n_prompt], w_uk.reshape(kv_lora, n_heads * nope), out_dtype=BF16, name="k_nope")
    v = _matmul(ckv_b[:n_prompt], w_uv.reshape(kv_lora, n_heads * v_dim), out_dtype=BF16, name="v_proj")
    attn_p = _prompt_attn(q, kn, krp_b, v, n_prompt, n_heads)

    w_uk_h = jnp.transpose(w_uk, (1, 0, 2))
    w_uv_h = jnp.transpose(w_uv, (1, 0, 2))
    q_lat = _q_latent(q, w_uk_h, n_prompt)
    cache_krp = jnp.pad(cache_krope, ((0, 0), (0, 0), (0, LANES - rope_dim)))
    o_lat = _sample_attn(q_lat, cache_ckv, cache_krp, ckv_b, krp_b, n_prompt, dec_seq)
    attn_s = _o_latent(o_lat, w_uv_h)
    return jnp.concatenate([attn_p, attn_s], axis=0), ckv, kr


def kernel(x_prompt, x_sample, cache_mla_ckv, cache_mla_krope, ln_mix_g, ln_mix_b, ln_ffn_g, ln_ffn_b,
           a_w_in, a_ln_v_g, a_ln_v_b, a_w_s, a_b_s, a_w_out,
           b_w_in, b_q_norm_g, b_kv_norm_g, b_w_uq, b_w_uk, b_w_uv, b_w_out,
           router_w, router_b, exp_w_gu, exp_b_gu, exp_w_down, exp_b_down):
    depth = ln_mix_g.shape[0]
    n_mixers = 2
    alpha = float((2 * depth) ** 0.25)
    bsz, seq, d = x_prompt.shape
    dec_b, dec_seq, _ = x_sample.shape
    n_prompt = bsz * seq
    n_sample = dec_b * dec_seq
    assert bsz == 1 and n_prompt % GMLP_CHUNK == 0 and GMLP_CHUNK % dec_seq == 0

    x = jnp.concatenate([x_prompt.reshape(n_prompt, d), x_sample.reshape(n_sample, d)], axis=0)
    xb = x.astype(BF16)
    v_s, ckv_p, kr_p, ckv_s, kr_s = [], [], [], [], []
    for i in range(depth):
        j = i // n_mixers
        if i % n_mixers == 0:
            u, v = _gmlp_in(xb, a_w_in[j])
            gated, vn = _gmlp_gate(u, v, a_ln_v_g[j], a_ln_v_b[j], a_w_s[j], a_b_s[j], n_prompt, dec_seq)
            pre = _matmul(gated, a_w_out[j], out_dtype=F32, res=x, alpha=alpha, name="gmlp_out")
            v_s.append(vn[n_prompt:].reshape(dec_b, dec_seq, -1))
        else:
            attn, ckv, kr = _mla_mixer(xb, n_prompt, dec_seq, cache_mla_ckv[j], cache_mla_krope[j],
                                       b_w_in[j], b_q_norm_g[j], b_kv_norm_g[j], b_w_uq[j], b_w_uk[j],
                                       b_w_uv[j])
            pre = _matmul(attn, b_w_out[j], out_dtype=F32, res=x, alpha=alpha, name="mla_out")
            ckv_p.append(ckv[:n_prompt].reshape(bsz, seq, -1))
            kr_p.append(kr[:n_prompt].reshape(bsz, seq, -1))
            ckv_s.append(ckv[n_prompt:].reshape(dec_b, dec_seq, -1))
            kr_s.append(kr[n_prompt:].reshape(dec_b, dec_seq, -1))
        x, xb = _moe_layer(pre, ln_mix_g[i], ln_mix_b[i], router_w[i], router_b[i], exp_w_gu, exp_b_gu,
                           exp_w_down, exp_b_down, i, ln_ffn_g[i], ln_ffn_b[i], alpha)

    return (x[:n_prompt].reshape(bsz, seq, d), x[n_prompt:].reshape(dec_b, dec_seq, d),
            jnp.stack(v_s), jnp.stack(ckv_p), jnp.stack(kr_p), jnp.stack(ckv_s), jnp.stack(kr_s))
```
